```python
import jax, jax.numpy as jnp
from jax import lax
import numpy as np

D_MODEL = 1024
BATCH = 2
SEQ = 8192
DEPTH = 2

HEAD_DIM = 128
HEADS_PER_GROUP = 4
DILATION_GROUPS = ((128, 1), (512, 4), (2048, 16))
N_GROUPS = 3
ATT_WIDTH = HEADS_PER_GROUP * HEAD_DIM
QKV_WIDTH = N_GROUPS * ATT_WIDTH
BAND_BLOCK = 128
LRU_WIDTH = D_MODEL
LRU_BLOCKS = 16
LRU_BLOCK_DIM = LRU_WIDTH // LRU_BLOCKS
CONV_WIDTH = 4
LRU_C = 8.0
N_BRANCHES = 2
NORM_EPS = 1e-6
NEG_INF = -1e30
SPLIT_SIZES = (QKV_WIDTH, QKV_WIDTH, QKV_WIDTH, ATT_WIDTH, LRU_WIDTH, LRU_WIDTH, N_BRANCHES * D_MODEL)
IN_WIDTH = 3 * QKV_WIDTH + ATT_WIDTH + 2 * LRU_WIDTH + N_BRANCHES * D_MODEL

kernel_name = "hybrid_dilated_attn_rglru_block"


def rms_norm(x, gain):
    xf = x.astype(jnp.float32)
    y = xf * lax.rsqrt(jnp.mean(xf * xf, axis=-1, keepdims=True) + NORM_EPS)
    return (y * gain.astype(jnp.float32)).astype(x.dtype)


def dilated_window_group(q, k, v, window, dilation):
    B, S, H, Dh = q.shape
    span = window // dilation
    L = S // dilation
    n_blk = -(-L // BAND_BLOCK)
    Lp = n_blk * BAND_BLOCK

    def to_sub(t):
        t = t.astype(jnp.float32).reshape(B, L, dilation, H, Dh).transpose(0, 2, 1, 3, 4)
        return jnp.pad(t, ((0, 0), (0, 0), (0, Lp - L), (0, 0), (0, 0)))

    def band(t):
        t = jnp.pad(to_sub(t), ((0, 0), (0, 0), (BAND_BLOCK, 0), (0, 0), (0, 0)))
        t = t.reshape(B, dilation, n_blk + 1, BAND_BLOCK, H, Dh)
        return jnp.concatenate([t[:, :, :-1], t[:, :, 1:]], axis=3)

    qs = to_sub(q).reshape(B, dilation, n_blk, BAND_BLOCK, H, Dh)
    kb = band(k)
    vb = band(v)
    s = jnp.einsum('brnqhd,brnkhd->brnhqk', qs, kb) * (Dh ** -0.5)
    qi = jnp.arange(BAND_BLOCK)[:, None] + BAND_BLOCK
    ki = jnp.arange(2 * BAND_BLOCK)[None, :]
    dist = qi - ki
    key_pos = jnp.arange(n_blk)[:, None, None] * BAND_BLOCK + ki[None] - BAND_BLOCK
    valid = (dist >= 0) & (dist <= span) & (key_pos >= 0)
    s = jnp.where(valid[None, None, :, None], s, NEG_INF)
    m = jnp.max(s, axis=-1)
    p = jnp.exp(s - m[..., None])
    l = jnp.sum(p, axis=-1)
    m = jnp.swapaxes(m, -1, -2)
    l = jnp.swapaxes(l, -1, -2)
    o = jnp.einsum('brnhqk,brnkhd->brnqhd', p, vb) / l[..., None]

    def from_sub(t):
        t = t.reshape((B, dilation, Lp) + t.shape[4:])[:, :, :L]
        t = jnp.swapaxes(t, 1, 2)
        return t.reshape((B, S) + t.shape[3:])

    return from_sub(o), from_sub(m), from_sub(l)


def dilated_attention(q, k, v):
    B, S, _ = q.shape
    shp = (B, S, N_GROUPS, HEADS_PER_GROUP, HEAD_DIM)
    qg, kg, vg = q.reshape(shp), k.reshape(shp), v.reshape(shp)
    outs, maxes, dens = [], [], []
    for g, (window, dilation) in enumerate(DILATION_GROUPS):
        o, m, l = dilated_window_group(qg[:, :, g], kg[:, :, g], vg[:, :, g], window, dilation)
        outs.append(o)
        maxes.append(m)
        dens.append(l)
    o = jnp.stack(outs, 0)
    m = jnp.stack(maxes, 0)
    l = jnp.stack(dens, 0)
    wgt = l * jnp.exp(m - jnp.max(m, axis=0, keepdims=True))
    o = jnp.sum(wgt[..., None] * o, axis=0) / jnp.sum(wgt, axis=0)[..., None]
    return o.reshape(B, S, ATT_WIDTH).astype(q.dtype)


def causal_depthwise_conv(u, w, b):
    S = u.shape[1]
    up = jnp.pad(u, ((0, 0), (CONV_WIDTH - 1, 0), (0, 0)))
    y = b
    for j in range(CONV_WIDTH):
        y = y + up[:, CONV_WIDTH - 1 - j: CONV_WIDTH - 1 - j + S] * w[j]
    return y


def rg_lru(u, w_rg, b_rg, w_ig, b_ig, lru_lambda):
    B, S, _ = u.shape
    uf = u.astype(jnp.float32)
    ub = uf.reshape(B, S, LRU_BLOCKS, LRU_BLOCK_DIM)
    r = jax.nn.sigmoid(jnp.einsum('bshi,hij->bshj', ub, w_rg.astype(jnp.float32)).reshape(B, S, LRU_WIDTH) + b_rg.astype(jnp.float32))
    i = jax.nn.sigmoid(jnp.einsum('bshi,hij->bshj', ub, w_ig.astype(jnp.float32)).reshape(B, S, LRU_WIDTH) + b_ig.astype(jnp.float32))
    log_a = -LRU_C * r * jax.nn.softplus(-lru_lambda.astype(jnp.float32))
    a = jnp.exp(log_a)
    xin = jnp.sqrt(-jnp.expm1(2.0 * log_a)) * (i * uf)

    def combine(left, right):
        a1, b1 = left
        a2, b2 = right
        return a1 * a2, a2 * b1 + b2

    _, h = lax.associative_scan(combine, (a, xin), axis=1)
    return h.astype(u.dtype)


def hybrid_layer(x, c, w_mod, b_mod, g_pre, w_in, conv_w, conv_b, w_rg, b_rg, w_ig, b_ig,
                 lru_lambda, w_pa, w_pb, w_o, g_post):
    mod = jax.nn.silu(c) @ w_mod + b_mod
    shift, scale, gate = jnp.split(mod[:, None, :], 3, axis=-1)
    h = rms_norm(x, g_pre) * (1 + scale) + shift
    proj = h @ w_in
    split_at = [int(v) for v in np.cumsum(SPLIT_SIZES)[:-1]]
    q, k, v, g_att, u, g_lru, merge = jnp.split(proj, split_at, axis=-1)
    y_a = (dilated_attention(q, k, v) * jax.nn.silu(g_att)) @ w_pa
    u = causal_depthwise_conv(u, conv_w, conv_b)
    y_b = (rg_lru(u, w_rg, b_rg, w_ig, b_ig, lru_lambda) * jax.nn.silu(g_lru)) @ w_pb
    gate_a, gate_b = jnp.split(jax.nn.sigmoid(merge), N_BRANCHES, axis=-1)
    out = (gate_a * y_a + gate_b * y_b) @ w_o
    return x + gate * rms_norm(out, g_post)


def setup_inputs(seed: int = 0) -> dict:
    key = jax.random.key(seed)
    ks = jax.random.split(key, 20)
    f32 = jnp.float32
    nrm = lambda k, shape, s: jax.random.normal(k, shape, f32) * s
    u = jax.random.uniform(ks[13], (DEPTH, LRU_WIDTH), f32, 0.9, 0.999)
    a_base = u ** (1.0 / LRU_C)
    lru_lambda = jnp.log(a_base) - jnp.log1p(-a_base)
    return {
        "x": nrm(ks[0], (BATCH, SEQ, D_MODEL), 1.0),
        "c": nrm(ks[1], (BATCH, D_MODEL), 1.0),
        "w_mod": nrm(ks[2], (DEPTH, D_MODEL, 3 * D_MODEL), 0.5 * D_MODEL ** -0.5),
        "b_mod": nrm(ks[3], (DEPTH, 3 * D_MODEL), 0.01),
        "g_pre": 1.0 + nrm(ks[4], (DEPTH, D_MODEL), 0.05),
        "w_in": nrm(ks[5], (DEPTH, D_MODEL, IN_WIDTH), D_MODEL ** -0.5),
        "conv_w": nrm(ks[6], (DEPTH, CONV_WIDTH, LRU_WIDTH), CONV_WIDTH ** -0.5),
        "conv_b": nrm(ks[7], (DEPTH, LRU_WIDTH), 0.01),
        "w_rg": nrm(ks[8], (DEPTH, LRU_BLOCKS, LRU_BLOCK_DIM, LRU_BLOCK_DIM), LRU_BLOCK_DIM ** -0.5),
        "b_rg": nrm(ks[9], (DEPTH, LRU_WIDTH), 0.01),
        "w_ig": nrm(ks[10], (DEPTH, LRU_BLOCKS, LRU_BLOCK_DIM, LRU_BLOCK_DIM), LRU_BLOCK_DIM ** -0.5),
        "b_ig": nrm(ks[11], (DEPTH, LRU_WIDTH), 0.01),
        "lru_lambda": lru_lambda,
        "w_pa": nrm(ks[14], (DEPTH, ATT_WIDTH, D_MODEL), ATT_WIDTH ** -0.5),
        "w_pb": nrm(ks[15], (DEPTH, LRU_WIDTH, D_MODEL), LRU_WIDTH ** -0.5),
        "w_o": nrm(ks[16], (DEPTH, D_MODEL, D_MODEL), D_MODEL ** -0.5),
        "g_post": 1.0 + nrm(ks[17], (DEPTH, D_MODEL), 0.05),
    }


def reference(x, c, w_mod, b_mod, g_pre, w_in, conv_w, conv_b, w_rg, b_rg, w_ig, b_ig,
              lru_lambda, w_pa, w_pb, w_o, g_post):
    for layer in range(DEPTH):
        x = hybrid_layer(x, c, w_mod[layer], b_mod[layer], g_pre[layer], w_in[layer],
                         conv_w[layer], conv_b[layer], w_rg[layer], b_rg[layer],
                         w_ig[layer], b_ig[layer], lru_lambda[layer], w_pa[layer],
                         w_pb[layer], w_o[layer], g_post[layer])
    return x
```

```python
import functools

import jax
import jax.numpy as jnp
from jax import lax
from jax.experimental import pallas as pl
from jax.experimental.pallas import tpu as pltpu

F32 = jnp.float32
BF16 = jnp.bfloat16

HEAD_DIM = 128
HEADS = 4
ATT_W = HEADS * HEAD_DIM
GROUPS = ((128, 1), (512, 4), (2048, 16))
N_GROUPS = 3
BAND = 128
LRU_BLOCKS = 16
CONV_W = 4
LRU_C = 8.0
EPS = 1e-6
NEG = -1e30

QKV_W = 3 * N_GROUPS * ATT_W
CHUNK = 512

VMEM_LIMIT = 56 * 1024 * 1024
MXU_TILE = 256


def _sigmoid(v):
    return 1.0 / (1.0 + jnp.exp(-v))


def _mod_kernel(ct_ref, w_ref, b_ref, o_ref):
    ct = ct_ref[...]
    sc = ct * _sigmoid(ct)
    w = w_ref[0]
    rows = []
    for b in range(ct.shape[1]):
        rows.append(jnp.sum(sc[:, b:b + 1] * w, axis=0, keepdims=True))
    o_ref[0] = jnp.concatenate(rows, axis=0) + b_ref[0]


def _mod_call(ct, w_mod, b_mod):
    depth, d, d3 = w_mod.shape
    nb = ct.shape[1]
    tn = 1024
    return pl.pallas_call(
        _mod_kernel,
        grid=(depth, d3 // tn),
        in_specs=[
            pl.BlockSpec((d, nb), lambda l, j: (0, 0)),
            pl.BlockSpec((1, d, tn), lambda l, j: (l, 0, j)),
            pl.BlockSpec((1, 1, tn), lambda l, j: (l, 0, j)),
        ],
        out_specs=pl.BlockSpec((1, nb, tn), lambda l, j: (l, 0, j)),
        out_shape=jax.ShapeDtypeStruct((depth, nb, d3), F32),
        compiler_params=pltpu.CompilerParams(
            dimension_semantics=("parallel", "parallel"),
            vmem_limit_bytes=VMEM_LIMIT),
        name="mod",
    )(ct, w_mod, b_mod.reshape(depth, 1, d3))


def _inproj_kernel(x_ref, mod_ref, gpre_ref, w_ref, qkv_ref, rest_ref, hb_ref):
    x = x_ref[...]
    ms = jnp.mean(x * x, axis=-1, keepdims=True)
    y = x * lax.rsqrt(ms + EPS) * gpre_ref[...]
    h = y * (1.0 + mod_ref[0, 1:2, :]) + mod_ref[0, 0:1, :]
    hb_ref[...] = h.astype(BF16)
    n_qkv = QKV_W // CHUNK
    n_all = w_ref.shape[1] // CHUNK
    for j in range(n_all):
        acc = jnp.dot(hb_ref[...], w_ref[:, j * CHUNK:(j + 1) * CHUNK],
                      preferred_element_type=F32)
        if j < n_qkv:
            qkv_ref[:, j * CHUNK:(j + 1) * CHUNK] = acc.astype(BF16)
        else:
            k = j - n_qkv
            if k == 0 or k in (3, 4):
                acc = acc * _sigmoid(acc)
            elif k >= 5:
                acc = _sigmoid(acc)
            dst = (n_all - n_qkv - 1) if k == 0 else k - 1
            rest_ref[:, dst * CHUNK:(dst + 1) * CHUNK] = acc.astype(BF16)


def _inproj_call(xf, mod_l, g_pre, w_in_b, tiles_per_batch, tm):
    n, d = xf.shape
    wn = w_in_b.shape[1]
    rest_w = wn - QKV_W
    return pl.pallas_call(
        _inproj_kernel,
        grid=(n // tm,),
        in_specs=[
            pl.BlockSpec((tm, d), lambda i: (i, 0)),
            pl.BlockSpec((1, 3, d), lambda i: (i // tiles_per_batch, 0, 0)),
            pl.BlockSpec((1, d), lambda i: (0, 0)),
            pl.BlockSpec((d, wn), lambda i: (0, 0), pipeline_mode=pl.Buffered(1)),
        ],
        out_specs=[
            pl.BlockSpec((tm, QKV_W), lambda i: (i, 0)),
            pl.BlockSpec((tm, rest_w), lambda i: (i, 0)),
        ],
        out_shape=[
            jax.ShapeDtypeStruct((n, QKV_W), BF16),
            jax.ShapeDtypeStruct((n, rest_w), BF16),
        ],
        scratch_shapes=[pltpu.VMEM((tm, d), BF16)],
        compiler_params=pltpu.CompilerParams(
            dimension_semantics=("parallel",),
            vmem_limit_bytes=VMEM_LIMIT),
        name="inproj",
    )(xf, mod_l, g_pre, w_in_b)


def _attn_kernel(q_ref, kp_ref, kc_ref, vp_ref, vc_ref, o_ref, lse_ref, *, tq):
    n = pl.program_id(2)
    scale = HEAD_DIM ** -0.5
    row = lax.broadcasted_iota(jnp.int32, (BAND, 2 * BAND), 0)
    col = lax.broadcasted_iota(jnp.int32, (BAND, 2 * BAND), 1)
    band = (col >= row) & (col <= row + BAND)
    first = band & ((col >= BAND) | (n > 0))
    lane = lax.broadcasted_iota(jnp.int32, (BAND, 128), 1)
    for i in range(tq // BAND):
        rs = slice(i * BAND, (i + 1) * BAND)
        lses = []
        for h in range(HEADS):
            hs = slice(h * HEAD_DIM, (h + 1) * HEAD_DIM)
            q = q_ref[0, rs, hs]
            if i == 0:
                kk = jnp.concatenate([kp_ref[0, :, hs], kc_ref[0, 0:BAND, hs]], axis=0)
                vv = jnp.concatenate([vp_ref[0, :, hs], vc_ref[0, 0:BAND, hs]], axis=0)
                mask = first
            else:
                kk = kc_ref[0, (i - 1) * BAND:(i + 1) * BAND, hs]
                vv = vc_ref[0, (i - 1) * BAND:(i + 1) * BAND, hs]
                mask = band
            s = lax.dot_general(q, kk, (((1,), (1,)), ((), ())),
                                preferred_element_type=F32) * scale
            s = jnp.where(mask, s, NEG)
            m = jnp.max(s, axis=-1, keepdims=True)
            p = jnp.exp(s - m)
            l = jnp.sum(p, axis=-1, keepdims=True)
            o = jnp.dot(p.astype(BF16), vv, preferred_element_type=F32)
            o_ref[0, rs, hs] = (o * (1.0 / l)).astype(BF16)
            lses.append(jnp.broadcast_to(m + jnp.log(l), (BAND, 128)))
        t = jnp.where(lane < 32, lses[0],
                      jnp.where(lane < 64, lses[1],
                                jnp.where(lane < 96, lses[2], lses[3])))
        lse_ref[0, rs, :] = t


def _attn_call(qkv, g, r, batch, seq, tq):
    n = qkv.shape[0]
    sub = seq // r
    qkv_v = qkv.reshape(batch, sub, r * QKV_W)
    ncol = QKV_W // ATT_W
    per = tq // BAND

    def qmap(b, rho, i):
        return (b, i, rho * ncol + g)

    def kcmap(b, rho, i):
        return (b, i, rho * ncol + N_GROUPS + g)

    def kpmap(b, rho, i):
        return (b, jnp.maximum(i * per - 1, 0), rho * ncol + N_GROUPS + g)

    def vcmap(b, rho, i):
        return (b, i, rho * ncol + 2 * N_GROUPS + g)

    def vpmap(b, rho, i):
        return (b, jnp.maximum(i * per - 1, 0), rho * ncol + 2 * N_GROUPS + g)

    o, lse = pl.pallas_call(
        functools.partial(_attn_kernel, tq=tq),
        grid=(batch, r, sub // tq),
        in_specs=[
            pl.BlockSpec((1, tq, ATT_W), qmap),
            pl.BlockSpec((1, BAND, ATT_W), kpmap),
            pl.BlockSpec((1, tq, ATT_W), kcmap),
            pl.BlockSpec((1, BAND, ATT_W), vpmap),
            pl.BlockSpec((1, tq, ATT_W), vcmap),
        ],
        out_specs=[
            pl.BlockSpec((1, tq, ATT_W), lambda b, rho, i: (b, i, rho)),
            pl.BlockSpec((1, tq, 128), lambda b, rho, i: (b, i, rho)),
        ],
        out_shape=[
            jax.ShapeDtypeStruct((batch, sub, r * ATT_W), BF16),
            jax.ShapeDtypeStruct((batch, sub, r * 128), F32),
        ],
        compiler_params=pltpu.CompilerParams(
            dimension_semantics=("parallel", "parallel", "parallel"),
            vmem_limit_bytes=VMEM_LIMIT),
        name=f"attn_g{g}",
    )(qkv_v, qkv_v, qkv_v, qkv_v, qkv_v)
    return o.reshape(n, ATT_W), lse.reshape(n, 128)


def _lru_kernel(u_ref, g_ref, cw_ref, cb_ref, wrg_ref, brg_ref, wig_ref, big_ref,
                lam_ref, hb_ref, ubuf, ucf, ucb, rpre, ipre, hcar, *, tm):
    s = pl.program_id(1)
    c = ucf.shape[1]

    @pl.when(s == 0)
    def _():
        ubuf[0:8, :] = jnp.zeros((8, c), F32)
        hcar[...] = jnp.zeros((8, c), F32)

    ubuf[8:8 + tm, :] = u_ref[...].astype(F32)
    cw = cw_ref[...]
    cb = cb_ref[...]
    rc = 64
    for k in range(tm // rc):
        base = 8 + k * rc
        acc = cb + cw[0:1, :] * ubuf[base:base + rc, :]
        for j in range(1, CONV_W):
            acc = acc + cw[j:j + 1, :] * ubuf[base - j:base - j + rc, :]
        ucf[k * rc:(k + 1) * rc, :] = acc
        ucb[k * rc:(k + 1) * rc, :] = acc.astype(BF16)
    ubuf[0:8, :] = ubuf[tm:tm + 8, :]

    for k in range(c // MXU_TILE):
        cs = slice(k * MXU_TILE, (k + 1) * MXU_TILE)
        lhs = ucb[:, cs]
        rpre[:, cs] = jnp.dot(lhs, wrg_ref[k], preferred_element_type=F32) + brg_ref[:, cs]
        ipre[:, cs] = jnp.dot(lhs, wig_ref[k], preferred_element_type=F32) + big_ref[:, cs]

    z = -lam_ref[...]
    softplus = jnp.maximum(z, 0.0) + jnp.log1p(jnp.exp(-jnp.abs(z)))
    nsp = -LRU_C * softplus
    rowi = lax.broadcasted_iota(jnp.int32, (8, c), 0)

    def body(j, hprev):
        r0 = pl.multiple_of(j * 16, 16)
        rg = _sigmoid(rpre[pl.ds(r0, 16), :])
        ig = _sigmoid(ipre[pl.ds(r0, 16), :])
        log_a = rg * nsp
        a = jnp.exp(log_a)
        om = -jnp.tanh(log_a) * (1.0 + a * a)
        bb = jnp.sqrt(om) * (ig * ucf[pl.ds(r0, 16), :])
        outs = []
        for half in range(2):
            ah = a[half * 8:(half + 1) * 8, :]
            bh = bb[half * 8:(half + 1) * 8, :]
            for d in (1, 2, 4):
                keep = rowi >= d
                a_sh = jnp.where(keep, pltpu.roll(ah, d, 0), 1.0)
                b_sh = jnp.where(keep, pltpu.roll(bh, d, 0), 0.0)
                bh = bh + ah * b_sh
                ah = ah * a_sh
            hh = bh + ah * hprev
            hprev = jnp.broadcast_to(hh[7:8, :], (8, c))
            outs.append(hh)
        h16 = jnp.concatenate(outs, axis=0)
        hb_ref[pl.ds(r0, 16), :] = (h16 * g_ref[pl.ds(r0, 16), :].astype(F32)).astype(BF16)
        return hprev

    hcar[...] = lax.fori_loop(0, tm // 16, body, hcar[...])


def _lru_call(rest, conv_w, conv_b, wrg_bd, b_rg, wig_bd, b_ig, lam, batch, seq, tm):
    n = rest.shape[0]
    c = conv_w.shape[1]
    spb = seq // tm
    row = lambda b, s: (b * spb + s, 0)
    grow = lambda b, s: (b * spb + s, 1)
    full2 = lambda b, s: (0, 0)
    full3 = lambda b, s: (0, 0, 0)
    nk = c // MXU_TILE
    return pl.pallas_call(
        functools.partial(_lru_kernel, tm=tm),
        grid=(batch, spb),
        in_specs=[
            pl.BlockSpec((tm, c), row),
            pl.BlockSpec((tm, c), grow),
            pl.BlockSpec((CONV_W, c), full2),
            pl.BlockSpec((1, c), full2),
            pl.BlockSpec((nk, MXU_TILE, MXU_TILE), full3),
            pl.BlockSpec((1, c), full2),
            pl.BlockSpec((nk, MXU_TILE, MXU_TILE), full3),
            pl.BlockSpec((1, c), full2),
            pl.BlockSpec((1, c), full2),
        ],
        out_specs=pl.BlockSpec((tm, c), row),
        out_shape=jax.ShapeDtypeStruct((n, c), BF16),
        scratch_shapes=[
            pltpu.VMEM((tm + 8, c), F32),
            pltpu.VMEM((tm, c), F32),
            pltpu.VMEM((tm, c), BF16),
            pltpu.VMEM((tm, c), F32),
            pltpu.VMEM((tm, c), F32),
            pltpu.VMEM((8, c), F32),
        ],
        compiler_params=pltpu.CompilerParams(
            dimension_semantics=("parallel", "arbitrary"),
            vmem_limit_bytes=VMEM_LIMIT),
        name="lru",
    )(rest, rest, conv_w, conv_b, wrg_bd, b_rg, wig_bd, b_ig, lam)


def _post_kernel(x_ref, mod_ref, o0_ref, o1_ref, o2_ref, l0_ref, l1_ref, l2_ref,
                 gatt_ref, hb_ref, mrg_ref, wpa_ref, wpb_ref, wo_ref, gpost_ref,
                 out_ref):
    tm = x_ref.shape[0]
    d = x_ref.shape[1]
    la, lb, lc = l0_ref[...], l1_ref[...], l2_ref[...]
    mx = jnp.maximum(jnp.maximum(la, lb), lc)
    ea, eb, ec = jnp.exp(la - mx), jnp.exp(lb - mx), jnp.exp(lc - mx)
    inv = 1.0 / (ea + eb + ec)
    ws = (ea * inv, eb * inv, ec * inv)
    orefs = (o0_ref, o1_ref, o2_ref)
    pieces = []
    for h in range(HEADS):
        hs = slice(h * HEAD_DIM, (h + 1) * HEAD_DIM)
        acc = None
        for w, o_ref in zip(ws, orefs):
            wb = jnp.broadcast_to(w[:, 32 * h:32 * h + 1], (tm, HEAD_DIM))
            t = wb * o_ref[:, hs].astype(F32)
            acc = t if acc is None else acc + t
        pieces.append((acc * gatt_ref[:, hs].astype(F32)).astype(BF16))
    ab = jnp.concatenate(pieces, axis=1)
    ya = jnp.dot(ab, wpa_ref[...], preferred_element_type=F32)
    yb = jnp.dot(hb_ref[...], wpb_ref[...], preferred_element_type=F32)
    zz = mrg_ref[:, 0:d].astype(F32) * ya + mrg_ref[:, d:2 * d].astype(F32) * yb
    out = jnp.dot(zz.astype(BF16), wo_ref[...], preferred_element_type=F32)
    ms = jnp.mean(out * out, axis=-1, keepdims=True)
    r = out * lax.rsqrt(ms + EPS) * gpost_ref[...]
    out_ref[...] = x_ref[...] + mod_ref[0, 2:3, :] * r


def _post_call(xf, mod_l, os_, lses, rest, hb, wpa_b, wpb_b, wo_b, g_post,
               tiles_per_batch, tm):
    n, d = xf.shape
    row = lambda i: (i, 0)
    mrg_blk = lambda i: (i, 1)
    gatt_blk = lambda i: (i, (4 * d) // ATT_W)
    full = lambda i: (0, 0)
    return pl.pallas_call(
        _post_kernel,
        grid=(n // tm,),
        in_specs=[
            pl.BlockSpec((tm, d), row),
            pl.BlockSpec((1, 3, d), lambda i: (i // tiles_per_batch, 0, 0)),
            pl.BlockSpec((tm, ATT_W), row),
            pl.BlockSpec((tm, ATT_W), row),
            pl.BlockSpec((tm, ATT_W), row),
            pl.BlockSpec((tm, 128), row),
            pl.BlockSpec((tm, 128), row),
            pl.BlockSpec((tm, 128), row),
            pl.BlockSpec((tm, ATT_W), gatt_blk),
            pl.BlockSpec((tm, d), row),
            pl.BlockSpec((tm, 2 * d), mrg_blk),
            pl.BlockSpec((ATT_W, d), full),
            pl.BlockSpec((d, d), full),
            pl.BlockSpec((d, d), full),
            pl.BlockSpec((1, d), full),
        ],
        out_specs=pl.BlockSpec((tm, d), row),
        out_shape=jax.ShapeDtypeStruct((n, d), F32),
        compiler_params=pltpu.CompilerParams(
            dimension_semantics=("parallel",),
            vmem_limit_bytes=VMEM_LIMIT),
        name="post",
    )(xf, mod_l, os_[0], os_[1], os_[2], lses[0], lses[1], lses[2],
      rest, hb, rest, wpa_b, wpb_b, wo_b, g_post)


def _block_diag(w):
    nb, bd, _ = w.shape
    per = MXU_TILE // bd
    w4 = w.reshape(nb // per, per, bd, bd)
    eye = jnp.eye(per, dtype=w.dtype)
    return jnp.einsum("cghj,gk->cghkj", w4, eye).reshape(nb // per, MXU_TILE, MXU_TILE)


def kernel(x, c, w_mod, b_mod, g_pre, w_in, conv_w, conv_b, w_rg, b_rg, w_ig, b_ig,
           lru_lambda, w_pa, w_pb, w_o, g_post):
    batch, seq, d = x.shape
    depth = w_mod.shape[0]
    n = batch * seq
    tm_in, tm_lru, tm_post, tq = 512, 512, 256, 512

    mod = _mod_call(c.T, w_mod, b_mod)
    xf = x.reshape(n, d)
    for l in range(depth):
        mod_l = mod[l].reshape(batch, 3, d)
        qkv, rest = _inproj_call(xf, mod_l, g_pre[l][None, :], w_in[l].astype(BF16),
                                 seq // tm_in, tm_in)
        os_, lses = [], []
        for g, (_, r) in enumerate(GROUPS):
            o, lse = _attn_call(qkv, g, r, batch, seq, tq)
            os_.append(o)
            lses.append(lse)
        hb = _lru_call(rest, conv_w[l], conv_b[l][None, :],
                       _block_diag(w_rg[l]).astype(BF16), b_rg[l][None, :],
                       _block_diag(w_ig[l]).astype(BF16), b_ig[l][None, :],
                       lru_lambda[l][None, :], batch, seq, tm_lru)
        xf = _post_call(xf, mod_l, os_, lses, rest, hb, w_pa[l].astype(BF16),
                        w_pb[l].astype(BF16), w_o[l].astype(BF16), g_post[l][None, :],
                        seq // tm_post, tm_post)
    return xf.reshape(batch, seq, d)
```

```python
import functools

import jax
import jax.numpy as jnp
from jax import lax
from jax.experimental import pallas as pl
from jax.experimental.pallas import tpu as pltpu

F32 = jnp.float32
BF16 = jnp.bfloat16

HEAD_DIM = 128
HEADS = 4
ATT_W = HEADS * HEAD_DIM
GROUPS = ((128, 1), (512, 4), (2048, 16))
N_GROUPS = 3
BAND = 128
LRU_BLOCKS = 16
CONV_W = 4
LRU_C = 8.0
EPS = 1e-6
NEG = -1e30

QKV_W = 3 * N_GROUPS * ATT_W
CHUNK = 512

VMEM_LIMIT = 56 * 1024 * 1024
MXU_TILE = 256


def _sigmoid(v):
    return 1.0 / (1.0 + jnp.exp(-v))


def _mod_kernel(ct_ref, w_ref, b_ref, o_ref):
    ct = ct_ref[...]
    sc = ct * _sigmoid(ct)
    w = w_ref[0]
    rows = []
    for b in range(ct.shape[1]):
        rows.append(jnp.sum(sc[:, b:b + 1] * w, axis=0, keepdims=True))
    o_ref[0] = jnp.concatenate(rows, axis=0) + b_ref[0]


def _mod_call(ct, w_mod, b_mod):
    depth, d, d3 = w_mod.shape
    nb = ct.shape[1]
    tn = 1024
    return pl.pallas_call(
        _mod_kernel,
        grid=(depth, d3 // tn),
        in_specs=[
            pl.BlockSpec((d, nb), lambda l, j: (0, 0)),
            pl.BlockSpec((1, d, tn), lambda l, j: (l, 0, j)),
            pl.BlockSpec((1, 1, tn), lambda l, j: (l, 0, j)),
        ],
        out_specs=pl.BlockSpec((1, nb, tn), lambda l, j: (l, 0, j)),
        out_shape=jax.ShapeDtypeStruct((depth, nb, d3), F32),
        compiler_params=pltpu.CompilerParams(
            dimension_semantics=("parallel", "parallel"),
            vmem_limit_bytes=VMEM_LIMIT),
        name="mod",
    )(ct, w_mod, b_mod.reshape(depth, 1, d3))


def _inproj_kernel(x_ref, mod_ref, gpre_ref, w_ref, a0_ref, a1_ref, a2_ref, rest_ref,
                   hb_ref, acc_ref):
    tm = x_ref.shape[0]
    x = x_ref[...]
    ms = jnp.mean(x * x, axis=-1, keepdims=True)
    y = x * lax.rsqrt(ms + EPS) * gpre_ref[...]
    h = y * (1.0 + mod_ref[0, 1:2, :]) + mod_ref[0, 0:1, :]
    hb_ref[...] = h.astype(BF16)
    n_qkv = QKV_W // CHUNK
    n_all = w_ref.shape[1] // CHUNK
    a_refs = (a0_ref, a1_ref, a2_ref)
    for j in range(n_all):
        acc = jnp.dot(hb_ref[...], w_ref[:, j * CHUNK:(j + 1) * CHUNK],
                      preferred_element_type=F32)
        if j < n_qkv:
            kind, g = divmod(j, N_GROUPS)
            r = GROUPS[g][1]
            if r == 1:
                a_refs[g][:, kind * ATT_W:(kind + 1) * ATT_W] = acc.astype(BF16)
            else:
                for sl in range(ATT_W // 128):
                    acc_ref[sl] = acc[:, sl * 128:(sl + 1) * 128]
                for rho in range(r):
                    c0 = (rho * 3 + kind) * ATT_W
                    for sl in range(ATT_W // 128):
                        v = acc_ref[sl, pl.ds(rho, tm // r, stride=r), :]
                        a_refs[g][:, c0 + sl * 128:c0 + (sl + 1) * 128] = v.astype(BF16)
        else:
            k = j - n_qkv
            if k == 0 or k in (3, 4):
                acc = acc * _sigmoid(acc)
            elif k >= 5:
                acc = _sigmoid(acc)
            dst = (n_all - n_qkv - 1) if k == 0 else k - 1
            rest_ref[:, dst * CHUNK:(dst + 1) * CHUNK] = acc.astype(BF16)


def _inproj_call(xf, mod_l, g_pre, w_in_b, tiles_per_batch, tm):
    n, d = xf.shape
    wn = w_in_b.shape[1]
    rest_w = wn - QKV_W
    return pl.pallas_call(
        _inproj_kernel,
        grid=(n // tm,),
        in_specs=[
            pl.BlockSpec((tm, d), lambda i: (i, 0)),
            pl.BlockSpec((1, 3, d), lambda i: (i // tiles_per_batch, 0, 0)),
            pl.BlockSpec((1, d), lambda i: (0, 0)),
            pl.BlockSpec((d, wn), lambda i: (0, 0), pipeline_mode=pl.Buffered(1)),
        ],
        out_specs=[
            pl.BlockSpec((tm // r, r * 3 * ATT_W), lambda i: (i, 0)) for _, r in GROUPS
        ] + [pl.BlockSpec((tm, rest_w), lambda i: (i, 0))],
        out_shape=[
            jax.ShapeDtypeStruct((n // r, r * 3 * ATT_W), BF16) for _, r in GROUPS
        ] + [jax.ShapeDtypeStruct((n, rest_w), BF16)],
        scratch_shapes=[pltpu.VMEM((tm, d), BF16),
                        pltpu.VMEM((ATT_W // 128, tm, 128), F32)],
        compiler_params=pltpu.CompilerParams(
            dimension_semantics=("parallel",),
            vmem_limit_bytes=VMEM_LIMIT),
        name="inproj",
    )(xf, mod_l, g_pre, w_in_b)


def _attn_kernel(q_ref, kp_ref, kc_ref, vp_ref, vc_ref, o_ref, lse_ref, *, tq):
    n = pl.program_id(2)
    scale = HEAD_DIM ** -0.5
    row = lax.broadcasted_iota(jnp.int32, (BAND, 2 * BAND), 0)
    col = lax.broadcasted_iota(jnp.int32, (BAND, 2 * BAND), 1)
    band = (col >= row) & (col <= row + BAND)
    first = band & ((col >= BAND) | (n > 0))
    lane = lax.broadcasted_iota(jnp.int32, (BAND, 128), 1)
    for i in range(tq // BAND):
        rs = slice(i * BAND, (i + 1) * BAND)
        lses = []
        for h in range(HEADS):
            hs = slice(h * HEAD_DIM, (h + 1) * HEAD_DIM)
            q = q_ref[rs, hs]
            if i == 0:
                kk = jnp.concatenate([kp_ref[:, hs], kc_ref[0:BAND, hs]], axis=0)
                vv = jnp.concatenate([vp_ref[:, hs], vc_ref[0:BAND, hs]], axis=0)
                mask = first
            else:
                kk = kc_ref[(i - 1) * BAND:(i + 1) * BAND, hs]
                vv = vc_ref[(i - 1) * BAND:(i + 1) * BAND, hs]
                mask = band
            s = lax.dot_general(q, kk, (((1,), (1,)), ((), ())),
                                preferred_element_type=F32) * scale
            s = jnp.where(mask, s, NEG)
            m = jnp.max(s, axis=-1, keepdims=True)
            p = jnp.exp(s - m)
            l = jnp.sum(p, axis=-1, keepdims=True)
            o = jnp.dot(p.astype(BF16), vv, preferred_element_type=F32)
            o_ref[rs, hs] = (o * (1.0 / l)).astype(BF16)
            lses.append(jnp.broadcast_to(m + jnp.log(l), (BAND, 128)))
        t = jnp.where(lane < 32, lses[0],
                      jnp.where(lane < 64, lses[1],
                                jnp.where(lane < 96, lses[2], lses[3])))
        lse_ref[rs, :] = t


def _attn_call(a, g, r, batch, seq, tq):
    sub = seq // r
    nblk = sub // tq
    per = tq // BAND

    def cur(kind):
        return lambda b, rho, i: (b * nblk + i, rho * 3 + kind)

    def prev(kind):
        return lambda b, rho, i: (b * nblk * per + jnp.maximum(i * per - 1, 0),
                                  rho * 3 + kind)

    return pl.pallas_call(
        functools.partial(_attn_kernel, tq=tq),
        grid=(batch, r, nblk),
        in_specs=[
            pl.BlockSpec((tq, ATT_W), cur(0)),
            pl.BlockSpec((BAND, ATT_W), prev(1)),
            pl.BlockSpec((tq, ATT_W), cur(1)),
            pl.BlockSpec((BAND, ATT_W), prev(2)),
            pl.BlockSpec((tq, ATT_W), cur(2)),
        ],
        out_specs=[
            pl.BlockSpec((tq, ATT_W), lambda b, rho, i: (b * nblk + i, rho)),
            pl.BlockSpec((tq, 128), lambda b, rho, i: (b * nblk + i, rho)),
        ],
        out_shape=[
            jax.ShapeDtypeStruct((batch * sub, r * ATT_W), BF16),
            jax.ShapeDtypeStruct((batch * sub, r * 128), F32),
        ],
        compiler_params=pltpu.CompilerParams(
            dimension_semantics=("parallel", "parallel", "parallel"),
            vmem_limit_bytes=VMEM_LIMIT),
        name=f"attn_g{g}",
    )(a, a, a, a, a)


def _lru_kernel(u_ref, g_ref, cw_ref, cb_ref, wrg_ref, brg_ref, wig_ref, big_ref,
                lam_ref, hb_ref, ubuf, ucf, ucb, rpre, ipre, hcar, *, tm):
    s = pl.program_id(1)
    c = ucf.shape[1]

    @pl.when(s == 0)
    def _():
        ubuf[0:8, :] = jnp.zeros((8, c), F32)
        hcar[...] = jnp.zeros((8, c), F32)

    ubuf[8:8 + tm, :] = u_ref[...].astype(F32)
    cw = cw_ref[...]
    cb = cb_ref[...]
    rc = 64
    for k in range(tm // rc):
        base = 8 + k * rc
        acc = cb + cw[0:1, :] * ubuf[base:base + rc, :]
        for j in range(1, CONV_W):
            acc = acc + cw[j:j + 1, :] * ubuf[base - j:base - j + rc, :]
        ucf[k * rc:(k + 1) * rc, :] = acc
        ucb[k * rc:(k + 1) * rc, :] = acc.astype(BF16)
    ubuf[0:8, :] = ubuf[tm:tm + 8, :]

    for k in range(c // MXU_TILE):
        cs = slice(k * MXU_TILE, (k + 1) * MXU_TILE)
        lhs = ucb[:, cs]
        rpre[:, cs] = jnp.dot(lhs, wrg_ref[k], preferred_element_type=F32) + brg_ref[:, cs]
        ipre[:, cs] = jnp.dot(lhs, wig_ref[k], preferred_element_type=F32) + big_ref[:, cs]

    z = -lam_ref[...]
    softplus = jnp.maximum(z, 0.0) + jnp.log1p(jnp.exp(-jnp.abs(z)))
    nsp = -LRU_C * softplus
    rowi = lax.broadcasted_iota(jnp.int32, (8, c), 0)

    def body(j, hprev):
        r0 = pl.multiple_of(j * 16, 16)
        rg = _sigmoid(rpre[pl.ds(r0, 16), :])
        ig = _sigmoid(ipre[pl.ds(r0, 16), :])
        log_a = rg * nsp
        a = jnp.exp(log_a)
        om = -jnp.tanh(log_a) * (1.0 + a * a)
        bb = jnp.sqrt(om) * (ig * ucf[pl.ds(r0, 16), :])
        outs = []
        for half in range(2):
            ah = a[half * 8:(half + 1) * 8, :]
            bh = bb[half * 8:(half + 1) * 8, :]
            for d in (1, 2, 4):
                keep = rowi >= d
                a_sh = jnp.where(keep, pltpu.roll(ah, d, 0), 1.0)
                b_sh = jnp.where(keep, pltpu.roll(bh, d, 0), 0.0)
                bh = bh + ah * b_sh
                ah = ah * a_sh
            hh = bh + ah * hprev
            hprev = jnp.broadcast_to(hh[7:8, :], (8, c))
            outs.append(hh)
        h16 = jnp.concatenate(outs, axis=0)
        hb_ref[pl.ds(r0, 16), :] = (h16 * g_ref[pl.ds(r0, 16), :].astype(F32)).astype(BF16)
        return hprev

    hcar[...] = lax.fori_loop(0, tm // 16, body, hcar[...])


def _lru_call(rest, conv_w, conv_b, wrg_bd, b_rg, wig_bd, b_ig, lam, batch, seq, tm):
    n = rest.shape[0]
    c = conv_w.shape[1]
    spb = seq // tm
    row = lambda b, s: (b * spb + s, 0)
    grow = lambda b, s: (b * spb + s, 1)
    full2 = lambda b, s: (0, 0)
    full3 = lambda b, s: (0, 0, 0)
    nk = c // MXU_TILE
    return pl.pallas_call(
        functools.partial(_lru_kernel, tm=tm),
        grid=(batch, spb),
        in_specs=[
            pl.BlockSpec((tm, c), row),
            pl.BlockSpec((tm, c), grow),
            pl.BlockSpec((CONV_W, c), full2),
            pl.BlockSpec((1, c), full2),
            pl.BlockSpec((nk, MXU_TILE, MXU_TILE), full3),
            pl.BlockSpec((1, c), full2),
            pl.BlockSpec((nk, MXU_TILE, MXU_TILE), full3),
            pl.BlockSpec((1, c), full2),
            pl.BlockSpec((1, c), full2),
        ],
        out_specs=pl.BlockSpec((tm, c), row),
        out_shape=jax.ShapeDtypeStruct((n, c), BF16),
        scratch_shapes=[
            pltpu.VMEM((tm + 8, c), F32),
            pltpu.VMEM((tm, c), F32),
            pltpu.VMEM((tm, c), BF16),
            pltpu.VMEM((tm, c), F32),
            pltpu.VMEM((tm, c), F32),
            pltpu.VMEM((8, c), F32),
        ],
        compiler_params=pltpu.CompilerParams(
            dimension_semantics=("parallel", "arbitrary"),
            vmem_limit_bytes=VMEM_LIMIT),
        name="lru",
    )(rest, rest, conv_w, conv_b, wrg_bd, b_rg, wig_bd, b_ig, lam)


def _post_kernel(x_ref, mod_ref, o0_ref, o1_ref, o2_ref, l0_ref, l1_ref, l2_ref,
                 gatt_ref, hb_ref, mrg_ref, wpa_ref, wpb_ref, wo_ref, gpost_ref,
                 out_ref, onat_ref, lnat_ref):
    tm = x_ref.shape[0]
    d = x_ref.shape[1]
    for g, (o_ref, l_ref) in enumerate(((o1_ref, l1_ref), (o2_ref, l2_ref))):
        r = GROUPS[g + 1][1]
        for rho in range(r):
            rows = pl.ds(rho, tm // r, stride=r)
            lnat_ref[g, rows, :] = l_ref[:, rho * 128:(rho + 1) * 128]
            for h in range(HEADS):
                c0 = rho * ATT_W + h * HEAD_DIM
                onat_ref[g, h, rows, :] = o_ref[:, c0:c0 + HEAD_DIM].astype(F32)
    la, lb, lc = l0_ref[...], lnat_ref[0], lnat_ref[1]
    mx = jnp.maximum(jnp.maximum(la, lb), lc)
    ea, eb, ec = jnp.exp(la - mx), jnp.exp(lb - mx), jnp.exp(lc - mx)
    inv = 1.0 / (ea + eb + ec)
    ws = (ea * inv, eb * inv, ec * inv)
    pieces = []
    for h in range(HEADS):
        hs = slice(h * HEAD_DIM, (h + 1) * HEAD_DIM)
        ogs = (o0_ref[:, hs].astype(F32), onat_ref[0, h], onat_ref[1, h])
        acc = None
        for w, og in zip(ws, ogs):
            wb = jnp.broadcast_to(w[:, 32 * h:32 * h + 1], (tm, HEAD_DIM))
            t = wb * og
            acc = t if acc is None else acc + t
        pieces.append((acc * gatt_ref[:, hs].astype(F32)).astype(BF16))
    ab = jnp.concatenate(pieces, axis=1)
    ya = jnp.dot(ab, wpa_ref[...], preferred_element_type=F32)
    yb = jnp.dot(hb_ref[...], wpb_ref[...], preferred_element_type=F32)
    zz = mrg_ref[:, 0:d].astype(F32) * ya + mrg_ref[:, d:2 * d].astype(F32) * yb
    out = jnp.dot(zz.astype(BF16), wo_ref[...], preferred_element_type=F32)
    ms = jnp.mean(out * out, axis=-1, keepdims=True)
    r = out * lax.rsqrt(ms + EPS) * gpost_ref[...]
    out_ref[...] = x_ref[...] + mod_ref[0, 2:3, :] * r


def _post_call(xf, mod_l, os_, lses, rest, hb, wpa_b, wpb_b, wo_b, g_post,
               tiles_per_batch, tm):
    n, d = xf.shape
    row = lambda i: (i, 0)
    mrg_blk = lambda i: (i, 1)
    gatt_blk = lambda i: (i, (4 * d) // ATT_W)
    full = lambda i: (0, 0)
    return pl.pallas_call(
        _post_kernel,
        grid=(n // tm,),
        in_specs=[
            pl.BlockSpec((tm, d), row),
            pl.BlockSpec((1, 3, d), lambda i: (i // tiles_per_batch, 0, 0)),
        ] + [pl.BlockSpec((tm // r, r * ATT_W), row) for _, r in GROUPS
        ] + [pl.BlockSpec((tm // r, r * 128), row) for _, r in GROUPS
        ] + [
            pl.BlockSpec((tm, ATT_W), gatt_blk),
            pl.BlockSpec((tm, d), row),
            pl.BlockSpec((tm, 2 * d), mrg_blk),
            pl.BlockSpec((ATT_W, d), full),
            pl.BlockSpec((d, d), full),
            pl.BlockSpec((d, d), full),
            pl.BlockSpec((1, d), full),
        ],
        out_specs=pl.BlockSpec((tm, d), row),
        out_shape=jax.ShapeDtypeStruct((n, d), F32),
        scratch_shapes=[pltpu.VMEM((N_GROUPS - 1, HEADS, tm, HEAD_DIM), F32),
                        pltpu.VMEM((N_GROUPS - 1, tm, 128), F32)],
        compiler_params=pltpu.CompilerParams(
            dimension_semantics=("parallel",),
            vmem_limit_bytes=VMEM_LIMIT),
        name="post",
    )(xf, mod_l, os_[0], os_[1], os_[2], lses[0], lses[1], lses[2],
      rest, hb, rest, wpa_b, wpb_b, wo_b, g_post)


def _block_diag(w):
    nb, bd, _ = w.shape
    per = MXU_TILE // bd
    w4 = w.reshape(nb // per, per, bd, bd)
    eye = jnp.eye(per, dtype=w.dtype)
    return jnp.einsum("cghj,gk->cghkj", w4, eye).reshape(nb // per, MXU_TILE, MXU_TILE)


def kernel(x, c, w_mod, b_mod, g_pre, w_in, conv_w, conv_b, w_rg, b_rg, w_ig, b_ig,
           lru_lambda, w_pa, w_pb, w_o, g_post):
    batch, seq, d = x.shape
    depth = w_mod.shape[0]
    n = batch * seq
    tm_in, tm_lru, tm_post, tq = 512, 512, 256, 512

    mod = _mod_call(c.T, w_mod, b_mod)
    xf = x.reshape(n, d)
    for l in range(depth):
        mod_l = mod[l].reshape(batch, 3, d)
        *a_g, rest = _inproj_call(xf, mod_l, g_pre[l][None, :], w_in[l].astype(BF16),
                                  seq // tm_in, tm_in)
        os_, lses = [], []
        for g, (_, r) in enumerate(GROUPS):
            o, lse = _attn_call(a_g[g], g, r, batch, seq, tq)
            os_.append(o)
            lses.append(lse)
        hb = _lru_call(rest, conv_w[l], conv_b[l][None, :],
                       _block_diag(w_rg[l]).astype(BF16), b_rg[l][None, :],
                       _block_diag(w_ig[l]).astype(BF16), b_ig[l][None, :],
                       lru_lambda[l][None, :], batch, seq, tm_lru)
        xf = _post_call(xf, mod_l, os_, lses, rest, hb, w_pa[l].astype(BF16),
                        w_pb[l].astype(BF16), w_o[l].astype(BF16), g_post[l][None, :],
                        seq // tm_post, tm_post)
    return xf.reshape(batch, seq, d)
```

```python
import functools

import jax
import jax.numpy as jnp
from jax import lax
from jax.experimental import pallas as pl
from jax.experimental.pallas import tpu as pltpu

F32 = jnp.float32
BF16 = jnp.bfloat16

HEAD_DIM = 128
HEADS = 4
ATT_W = HEADS * HEAD_DIM
GROUPS = ((128, 1), (512, 4), (2048, 16))
N_GROUPS = 3
BAND = 128
LRU_BLOCKS = 16
CONV_W = 4
LRU_C = 8.0
EPS = 1e-6
NEG = -1e30

QKV_W = 3 * N_GROUPS * ATT_W
CHUNK = 512

VMEM_LIMIT = 58 * 1024 * 1024
MXU_TILE = 256


def _sigmoid(v):
    return 1.0 / (1.0 + jnp.exp(-v))


def _mod_kernel(ct_ref, w_ref, b_ref, o_ref):
    ct = ct_ref[...]
    sc = ct * _sigmoid(ct)
    w = w_ref[0]
    rows = []
    for b in range(ct.shape[1]):
        rows.append(jnp.sum(sc[:, b:b + 1] * w, axis=0, keepdims=True))
    o_ref[0] = jnp.concatenate(rows, axis=0) + b_ref[0]


def _mod_call(ct, w_mod, b_mod):
    depth, d, d3 = w_mod.shape
    nb = ct.shape[1]
    tn = 1024
    return pl.pallas_call(
        _mod_kernel,
        grid=(depth, d3 // tn),
        in_specs=[
            pl.BlockSpec((d, nb), lambda l, j: (0, 0)),
            pl.BlockSpec((1, d, tn), lambda l, j: (l, 0, j)),
            pl.BlockSpec((1, 1, tn), lambda l, j: (l, 0, j)),
        ],
        out_specs=pl.BlockSpec((1, nb, tn), lambda l, j: (l, 0, j)),
        out_shape=jax.ShapeDtypeStruct((depth, nb, d3), F32),
        compiler_params=pltpu.CompilerParams(
            dimension_semantics=("parallel", "parallel"),
            vmem_limit_bytes=VMEM_LIMIT),
        name="mod",
    )(ct, w_mod, b_mod.reshape(depth, 1, d3))


J_GATT = 9
J_U = (10, 11)
J_GLRU = (12, 13)
J_MERGE = (14, 15, 16, 17)
SCAN_ROWS = 16


def _inproj_lru_kernel(x_ref, mod_ref, gpre_ref, w_ref, cw_ref, cb_ref, wrg_ref, brg_ref,
                       wig_ref, big_ref, lam_ref,
                       a0_ref, a1_ref, a2_ref, rest_ref, hbo_ref,
                       hb_ref, acc_ref, ubuf, ucf, rpre, ipre, glru, hcar,
                       *, tiles_per_batch):
    tm = x_ref.shape[0]
    c = ucf.shape[1]

    @pl.when(pl.program_id(0) % tiles_per_batch == 0)
    def _():
        ubuf[0:8, :] = jnp.zeros((8, c), F32)
        hcar[...] = jnp.zeros((8, c), F32)

    x = x_ref[...]
    ms = jnp.mean(x * x, axis=-1, keepdims=True)
    y = x * lax.rsqrt(ms + EPS) * gpre_ref[...]
    h = y * (1.0 + mod_ref[0, 1:2, :]) + mod_ref[0, 0:1, :]
    hb_ref[...] = h.astype(BF16)

    def proj(j):
        return jnp.dot(hb_ref[...], w_ref[:, j * CHUNK:(j + 1) * CHUNK],
                       preferred_element_type=F32)

    a_refs = (a0_ref, a1_ref, a2_ref)

    def qkv_task(j):
        acc = proj(j)
        kind, g = divmod(j, N_GROUPS)
        r = GROUPS[g][1]
        if r == 1:
            a_refs[g][:, kind * ATT_W:(kind + 1) * ATT_W] = acc.astype(BF16)
        else:
            for sl in range(ATT_W // 128):
                acc_ref[sl] = acc[:, sl * 128:(sl + 1) * 128]
            for rho in range(r):
                c0 = (rho * 3 + kind) * ATT_W
                for sl in range(ATT_W // 128):
                    v = acc_ref[sl, pl.ds(rho, tm // r, stride=r), :]
                    a_refs[g][:, c0 + sl * 128:c0 + (sl + 1) * 128] = v.astype(BF16)

    def merge_task(t, j):
        rest_ref[:, t * CHUNK:(t + 1) * CHUNK] = _sigmoid(proj(j)).astype(BF16)

    def gatt_task():
        acc = proj(J_GATT)
        nm = len(J_MERGE)
        rest_ref[:, nm * CHUNK:(nm + 1) * CHUNK] = (acc * _sigmoid(acc)).astype(BF16)

    tasks = [functools.partial(qkv_task, j) for j in range(J_GATT)]
    tasks += [functools.partial(merge_task, t, j) for t, j in enumerate(J_MERGE)]
    tasks.append(gatt_task)


    for t, j in enumerate(J_U):
        ubuf[8:8 + tm, t * CHUNK:(t + 1) * CHUNK] = proj(j)
    for _ in range(2):
        tasks.pop(0)()
    cw = cw_ref[...]
    cb = cb_ref[...]
    rc = 64
    for k in range(tm // rc):
        base = 8 + k * rc
        acc = cb + cw[0:1, :] * ubuf[base:base + rc, :]
        for t in range(1, CONV_W):
            acc = acc + cw[t:t + 1, :] * ubuf[base - t:base - t + rc, :]
        ucf[k * rc:(k + 1) * rc, :] = acc
    ubuf[0:8, :] = ubuf[tm:tm + 8, :]

    for k in range(c // MXU_TILE):
        cs = slice(k * MXU_TILE, (k + 1) * MXU_TILE)
        lhs = ucf[:, cs].astype(BF16)
        rpre[:, cs] = jnp.dot(lhs, wrg_ref[k], preferred_element_type=F32) + brg_ref[:, cs]
        ipre[:, cs] = jnp.dot(lhs, wig_ref[k], preferred_element_type=F32) + big_ref[:, cs]
        tasks.pop(0)()

    for t, j in enumerate(J_GLRU):
        acc = proj(j)
        glru[:, t * CHUNK:(t + 1) * CHUNK] = acc * _sigmoid(acc)

    z = -lam_ref[...]
    softplus = jnp.maximum(z, 0.0) + jnp.log1p(jnp.exp(-jnp.abs(z)))
    nsp = -LRU_C * softplus
    rowi = lax.broadcasted_iota(jnp.int32, (8, c), 0)
    keeps = {d: rowi >= d for d in (1, 2, 4)}

    n_scan = tm // SCAN_ROWS
    every = max(1, n_scan // len(tasks))
    hprev = hcar[...]
    for j in range(n_scan):
        if j % every == every - 1 and tasks:
            tasks.pop(0)()
        rs = slice(j * SCAN_ROWS, (j + 1) * SCAN_ROWS)
        rg = _sigmoid(rpre[rs, :])
        ig = _sigmoid(ipre[rs, :])
        log_a = rg * nsp
        a = jnp.exp(log_a)
        om = -jnp.tanh(log_a) * (1.0 + a * a)
        bb = jnp.sqrt(om) * (ig * ucf[rs, :])
        outs = []
        for half in range(SCAN_ROWS // 8):
            ah = a[half * 8:(half + 1) * 8, :]
            bh = bb[half * 8:(half + 1) * 8, :]
            for d in (1, 2, 4):
                a_sh = jnp.where(keeps[d], pltpu.roll(ah, d, 0), 1.0)
                b_sh = jnp.where(keeps[d], pltpu.roll(bh, d, 0), 0.0)
                bh = bh + ah * b_sh
                ah = ah * a_sh
            hh = bh + ah * hprev
            hprev = jnp.broadcast_to(hh[7:8, :], (8, c))
            outs.append(hh)
        h16 = jnp.concatenate(outs, axis=0)
        hbo_ref[rs, :] = (h16 * glru[rs, :]).astype(BF16)
    hcar[...] = hprev
    for task in tasks:
        task()


def _inproj_lru_call(xf, mod_l, g_pre, w_in_b, conv_w, conv_b, wrg_bd, b_rg, wig_bd, b_ig,
                     lam, tiles_per_batch, tm):
    n, d = xf.shape
    wn = w_in_b.shape[1]
    c = conv_w.shape[1]
    rest_w = (len(J_MERGE) + 1) * CHUNK
    nk = c // MXU_TILE
    row = lambda i: (i, 0)
    full2 = lambda i: (0, 0)
    full3 = lambda i: (0, 0, 0)
    return pl.pallas_call(
        functools.partial(_inproj_lru_kernel, tiles_per_batch=tiles_per_batch),
        grid=(n // tm,),
        in_specs=[
            pl.BlockSpec((tm, d), row),
            pl.BlockSpec((1, 3, d), lambda i: (i // tiles_per_batch, 0, 0)),
            pl.BlockSpec((1, d), full2),
            pl.BlockSpec((d, wn), full2, pipeline_mode=pl.Buffered(1)),
            pl.BlockSpec((CONV_W, c), full2),
            pl.BlockSpec((1, c), full2),
            pl.BlockSpec((nk, MXU_TILE, MXU_TILE), full3),
            pl.BlockSpec((1, c), full2),
            pl.BlockSpec((nk, MXU_TILE, MXU_TILE), full3),
            pl.BlockSpec((1, c), full2),
            pl.BlockSpec((1, c), full2),
        ],
        out_specs=[
            pl.BlockSpec((tm // r, r * 3 * ATT_W), row) for _, r in GROUPS
        ] + [pl.BlockSpec((tm, rest_w), row), pl.BlockSpec((tm, c), row)],
        out_shape=[
            jax.ShapeDtypeStruct((n // r, r * 3 * ATT_W), BF16) for _, r in GROUPS
        ] + [jax.ShapeDtypeStruct((n, rest_w), BF16), jax.ShapeDtypeStruct((n, c), BF16)],
        scratch_shapes=[
            pltpu.VMEM((tm, d), BF16),
            pltpu.VMEM((ATT_W // 128, tm, 128), F32),
            pltpu.VMEM((tm + 8, c), F32),
            pltpu.VMEM((tm, c), F32),
            pltpu.VMEM((tm, c), F32),
            pltpu.VMEM((tm, c), F32),
            pltpu.VMEM((tm, c), F32),
            pltpu.VMEM((8, c), F32),
        ],
        compiler_params=pltpu.CompilerParams(
            dimension_semantics=("arbitrary",),
            vmem_limit_bytes=VMEM_LIMIT),
        name="inproj_lru",
    )(xf, mod_l, g_pre, w_in_b, conv_w, conv_b, wrg_bd, b_rg, wig_bd, b_ig, lam)


def _attn_kernel(q_ref, kp_ref, kc_ref, vp_ref, vc_ref, o_ref, lse_ref, *, tq):
    n = pl.program_id(2)
    scale = HEAD_DIM ** -0.5
    row = lax.broadcasted_iota(jnp.int32, (BAND, 2 * BAND), 0)
    col = lax.broadcasted_iota(jnp.int32, (BAND, 2 * BAND), 1)
    band = (col >= row) & (col <= row + BAND)
    first = band & ((col >= BAND) | (n > 0))
    lane = lax.broadcasted_iota(jnp.int32, (BAND, 128), 1)
    for i in range(tq // BAND):
        rs = slice(i * BAND, (i + 1) * BAND)
        lses = []
        for h in range(HEADS):
            hs = slice(h * HEAD_DIM, (h + 1) * HEAD_DIM)
            q = q_ref[rs, hs]
            if i == 0:
                kk = jnp.concatenate([kp_ref[:, hs], kc_ref[0:BAND, hs]], axis=0)
                vv = jnp.concatenate([vp_ref[:, hs], vc_ref[0:BAND, hs]], axis=0)
                mask = first
            else:
                kk = kc_ref[(i - 1) * BAND:(i + 1) * BAND, hs]
                vv = vc_ref[(i - 1) * BAND:(i + 1) * BAND, hs]
                mask = band
            s = lax.dot_general(q, kk, (((1,), (1,)), ((), ())),
                                preferred_element_type=F32) * scale
            s = jnp.where(mask, s, NEG)
            m = jnp.max(s, axis=-1, keepdims=True)
            p = jnp.exp(s - m)
            l = jnp.sum(p, axis=-1, keepdims=True)
            o = jnp.dot(p.astype(BF16), vv, preferred_element_type=F32)
            o_ref[rs, hs] = (o * (1.0 / l)).astype(BF16)
            lses.append(jnp.broadcast_to(m + jnp.log(l), (BAND, 128)))
        t = jnp.where(lane < 32, lses[0],
                      jnp.where(lane < 64, lses[1],
                                jnp.where(lane < 96, lses[2], lses[3])))
        lse_ref[rs, :] = t


def _attn_call(a, g, r, batch, seq, tq):
    sub = seq // r
    nblk = sub // tq
    per = tq // BAND

    def cur(kind):
        return lambda b, rho, i: (b * nblk + i, rho * 3 + kind)

    def prev(kind):
        return lambda b, rho, i: (b * nblk * per + jnp.maximum(i * per - 1, 0),
                                  rho * 3 + kind)

    return pl.pallas_call(
        functools.partial(_attn_kernel, tq=tq),
        grid=(batch, r, nblk),
        in_specs=[
            pl.BlockSpec((tq, ATT_W), cur(0)),
            pl.BlockSpec((BAND, ATT_W), prev(1)),
            pl.BlockSpec((tq, ATT_W), cur(1)),
            pl.BlockSpec((BAND, ATT_W), prev(2)),
            pl.BlockSpec((tq, ATT_W), cur(2)),
        ],
        out_specs=[
            pl.BlockSpec((tq, ATT_W), lambda b, rho, i: (b * nblk + i, rho)),
            pl.BlockSpec((tq, 128), lambda b, rho, i: (b * nblk + i, rho)),
        ],
        out_shape=[
            jax.ShapeDtypeStruct((batch * sub, r * ATT_W), BF16),
            jax.ShapeDtypeStruct((batch * sub, r * 128), F32),
        ],
        compiler_params=pltpu.CompilerParams(
            dimension_semantics=("parallel", "parallel", "parallel"),
            vmem_limit_bytes=VMEM_LIMIT),
        name=f"attn_g{g}",
    )(a, a, a, a, a)


def _post_kernel(x_ref, mod_ref, o0_ref, o1_ref, o2_ref, l0_ref, l1_ref, l2_ref,
                 gatt_ref, hb_ref, mrg_ref, wpa_ref, wpb_ref, wo_ref, gpost_ref,
                 out_ref, onat_ref, lnat_ref):
    tm = x_ref.shape[0]
    d = x_ref.shape[1]
    for g, (o_ref, l_ref) in enumerate(((o1_ref, l1_ref), (o2_ref, l2_ref))):
        r = GROUPS[g + 1][1]
        for rho in range(r):
            rows = pl.ds(rho, tm // r, stride=r)
            lnat_ref[g, rows, :] = l_ref[:, rho * 128:(rho + 1) * 128]
            for h in range(HEADS):
                c0 = rho * ATT_W + h * HEAD_DIM
                onat_ref[g, h, rows, :] = o_ref[:, c0:c0 + HEAD_DIM].astype(F32)
    la, lb, lc = l0_ref[...], lnat_ref[0], lnat_ref[1]
    mx = jnp.maximum(jnp.maximum(la, lb), lc)
    ea, eb, ec = jnp.exp(la - mx), jnp.exp(lb - mx), jnp.exp(lc - mx)
    inv = 1.0 / (ea + eb + ec)
    ws = (ea * inv, eb * inv, ec * inv)
    pieces = []
    for h in range(HEADS):
        hs = slice(h * HEAD_DIM, (h + 1) * HEAD_DIM)
        ogs = (o0_ref[:, hs].astype(F32), onat_ref[0, h], onat_ref[1, h])
        acc = None
        for w, og in zip(ws, ogs):
            wb = jnp.broadcast_to(w[:, 32 * h:32 * h + 1], (tm, HEAD_DIM))
            t = wb * og
            acc = t if acc is None else acc + t
        pieces.append((acc * gatt_ref[:, hs].astype(F32)).astype(BF16))
    ab = jnp.concatenate(pieces, axis=1)
    ya = jnp.dot(ab, wpa_ref[...], preferred_element_type=F32)
    yb = jnp.dot(hb_ref[...], wpb_ref[...], preferred_element_type=F32)
    zz = mrg_ref[:, 0:d].astype(F32) * ya + mrg_ref[:, d:2 * d].astype(F32) * yb
    out = jnp.dot(zz.astype(BF16), wo_ref[...], preferred_element_type=F32)
    ms = jnp.mean(out * out, axis=-1, keepdims=True)
    r = out * lax.rsqrt(ms + EPS) * gpost_ref[...]
    out_ref[...] = x_ref[...] + mod_ref[0, 2:3, :] * r


def _post_call(xf, mod_l, os_, lses, rest, hb, wpa_b, wpb_b, wo_b, g_post,
               tiles_per_batch, tm):
    n, d = xf.shape
    row = lambda i: (i, 0)
    gatt_blk = lambda i: (i, (2 * d) // ATT_W)
    full = lambda i: (0, 0)
    return pl.pallas_call(
        _post_kernel,
        grid=(n // tm,),
        in_specs=[
            pl.BlockSpec((tm, d), row),
            pl.BlockSpec((1, 3, d), lambda i: (i // tiles_per_batch, 0, 0)),
        ] + [pl.BlockSpec((tm // r, r * ATT_W), row) for _, r in GROUPS
        ] + [pl.BlockSpec((tm // r, r * 128), row) for _, r in GROUPS
        ] + [
            pl.BlockSpec((tm, ATT_W), gatt_blk),
            pl.BlockSpec((tm, d), row),
            pl.BlockSpec((tm, 2 * d), row),
            pl.BlockSpec((ATT_W, d), full),
            pl.BlockSpec((d, d), full),
            pl.BlockSpec((d, d), full),
            pl.BlockSpec((1, d), full),
        ],
        out_specs=pl.BlockSpec((tm, d), row),
        out_shape=jax.ShapeDtypeStruct((n, d), F32),
        scratch_shapes=[pltpu.VMEM((N_GROUPS - 1, HEADS, tm, HEAD_DIM), F32),
                        pltpu.VMEM((N_GROUPS - 1, tm, 128), F32)],
        compiler_params=pltpu.CompilerParams(
            dimension_semantics=("parallel",),
            vmem_limit_bytes=VMEM_LIMIT),
        name="post",
    )(xf, mod_l, os_[0], os_[1], os_[2], lses[0], lses[1], lses[2],
      rest, hb, rest, wpa_b, wpb_b, wo_b, g_post)


def _block_diag(w):
    nb, bd, _ = w.shape
    per = MXU_TILE // bd
    w4 = w.reshape(nb // per, per, bd, bd)
    eye = jnp.eye(per, dtype=w.dtype)
    return jnp.einsum("cghj,gk->cghkj", w4, eye).reshape(nb // per, MXU_TILE, MXU_TILE)


def kernel(x, c, w_mod, b_mod, g_pre, w_in, conv_w, conv_b, w_rg, b_rg, w_ig, b_ig,
           lru_lambda, w_pa, w_pb, w_o, g_post):
    batch, seq, d = x.shape
    depth = w_mod.shape[0]
    n = batch * seq
    tm_in, tm_post, tq = 512, 256, 512

    mod = _mod_call(c.T, w_mod, b_mod)
    xf = x.reshape(n, d)
    for l in range(depth):
        mod_l = mod[l].reshape(batch, 3, d)
        *a_g, rest, hb = _inproj_lru_call(
            xf, mod_l, g_pre[l][None, :], w_in[l].astype(BF16), conv_w[l], conv_b[l][None, :],
            _block_diag(w_rg[l]).astype(BF16), b_rg[l][None, :],
            _block_diag(w_ig[l]).astype(BF16), b_ig[l][None, :],
            lru_lambda[l][None, :], seq // tm_in, tm_in)
        os_, lses = [], []
        for g, (_, r) in enumerate(GROUPS):
            o, lse = _attn_call(a_g[g], g, r, batch, seq, tq)
            os_.append(o)
            lses.append(lse)
        xf = _post_call(xf, mod_l, os_, lses, rest, hb, w_pa[l].astype(BF16),
                        w_pb[l].astype(BF16), w_o[l].astype(BF16), g_post[l][None, :],
                        seq // tm_post, tm_post)
    return xf.reshape(batch, seq, d)
```

```python
import functools

import jax
import jax.numpy as jnp
from jax import lax
from jax.experimental import pallas as pl
from jax.experimental.pallas import tpu as pltpu

F32 = jnp.float32
BF16 = jnp.bfloat16

HEAD_DIM = 128
HEADS = 4
ATT_W = HEADS * HEAD_DIM
GROUPS = ((128, 1), (512, 4), (2048, 16))
N_GROUPS = 3
BAND = 128
LRU_BLOCKS = 16
CONV_W = 4
LRU_C = 8.0
EPS = 1e-6
NEG = -1e30

QKV_W = 3 * N_GROUPS * ATT_W
CHUNK = 512

VMEM_LIMIT = 58 * 1024 * 1024
MXU_TILE = 256


def _sigmoid(v):
    return 1.0 / (1.0 + jnp.exp(-v))


def _mod_kernel(ct_ref, w_ref, b_ref, o_ref):
    ct = ct_ref[...]
    sc = ct * _sigmoid(ct)
    w = w_ref[0]
    rows = []
    for b in range(ct.shape[1]):
        rows.append(jnp.sum(sc[:, b:b + 1] * w, axis=0, keepdims=True))
    o_ref[0] = jnp.concatenate(rows, axis=0) + b_ref[0]


def _mod_call(ct, w_mod, b_mod):
    depth, d, d3 = w_mod.shape
    nb = ct.shape[1]
    tn = 1024
    return pl.pallas_call(
        _mod_kernel,
        grid=(depth, d3 // tn),
        in_specs=[
            pl.BlockSpec((d, nb), lambda l, j: (0, 0)),
            pl.BlockSpec((1, d, tn), lambda l, j: (l, 0, j)),
            pl.BlockSpec((1, 1, tn), lambda l, j: (l, 0, j)),
        ],
        out_specs=pl.BlockSpec((1, nb, tn), lambda l, j: (l, 0, j)),
        out_shape=jax.ShapeDtypeStruct((depth, nb, d3), F32),
        compiler_params=pltpu.CompilerParams(
            dimension_semantics=("parallel", "parallel"),
            vmem_limit_bytes=VMEM_LIMIT),
        name="mod",
    )(ct, w_mod, b_mod.reshape(depth, 1, d3))


J_GATT = 9
J_U = (10, 11)
J_GLRU = (12, 13)
J_MERGE = (14, 15, 16, 17)
SCAN_STEPS = 16


def _inproj_lru_kernel(x_ref, mod_ref, gpre_ref, w_ref, cw_ref, cb_ref, wrg_ref, brg_ref,
                       wig_ref, big_ref, lam_ref,
                       a0_ref, a1_ref, a2_ref, rest_ref, hbo_ref,
                       hb_ref, acc_ref, ubuf, ucb, uct, rpt, ipt, glru, hcar,
                       *, tiles_per_batch):
    tm = x_ref.shape[0]
    c = ucb.shape[1]

    @pl.when(pl.program_id(0) % tiles_per_batch == 0)
    def _():
        ubuf[0:8, :] = jnp.zeros((8, c), F32)
        hcar[...] = jnp.zeros(hcar.shape, F32)

    x = x_ref[...]
    ms = jnp.mean(x * x, axis=-1, keepdims=True)
    y = x * lax.rsqrt(ms + EPS) * gpre_ref[...]
    h = y * (1.0 + mod_ref[0, 1:2, :]) + mod_ref[0, 0:1, :]
    hb_ref[...] = h.astype(BF16)

    def proj(j):
        return jnp.dot(hb_ref[...], w_ref[:, j * CHUNK:(j + 1) * CHUNK],
                       preferred_element_type=F32)

    a_refs = (a0_ref, a1_ref, a2_ref)

    def tail(acc):
        return acc[tm - 8:tm, CHUNK - 128:CHUNK]

    def qkv_task(j):
        acc = proj(j)
        kind, g = divmod(j, N_GROUPS)
        r = GROUPS[g][1]
        if r == 1:
            a_refs[g][:, kind * ATT_W:(kind + 1) * ATT_W] = acc.astype(BF16)
        else:
            for sl in range(ATT_W // 128):
                acc_ref[sl] = acc[:, sl * 128:(sl + 1) * 128]
            for rho in range(r):
                c0 = (rho * 3 + kind) * ATT_W
                for sl in range(ATT_W // 128):
                    v = acc_ref[sl, pl.ds(rho, tm // r, stride=r), :]
                    a_refs[g][:, c0 + sl * 128:c0 + (sl + 1) * 128] = v.astype(BF16)
        return tail(acc)

    def merge_task(t, j):
        acc = proj(j)
        rest_ref[:, t * CHUNK:(t + 1) * CHUNK] = _sigmoid(acc).astype(BF16)
        return tail(acc)

    def gatt_task():
        acc = proj(J_GATT)
        nm = len(J_MERGE)
        rest_ref[:, nm * CHUNK:(nm + 1) * CHUNK] = (acc * _sigmoid(acc)).astype(BF16)
        return tail(acc)

    simple = [functools.partial(merge_task, t, j) for t, j in enumerate(J_MERGE)] + [gatt_task]
    tasks = []
    for kind in range(3):
        for g in (1, 0, 2):
            tasks.append(functools.partial(qkv_task, kind * N_GROUPS + g))
        tasks.append(simple.pop(0))
    tasks += simple


    for t, j in enumerate(J_U):
        ubuf[8:8 + tm, t * CHUNK:(t + 1) * CHUNK] = proj(j)
    for _ in range(2):
        tasks.pop(0)()

    lb = c // 128
    cw = cw_ref[...]
    cb = cb_ref[...]
    rc = 64
    for k in range(tm // rc):
        base = 8 + k * rc
        acc = cb + cw[0:1, :] * ubuf[base:base + rc, :]
        for t in range(1, CONV_W):
            acc = acc + cw[t:t + 1, :] * ubuf[base - t:base - t + rc, :]
        ucb[k * rc:(k + 1) * rc, :] = acc.astype(BF16)
        for q in range(lb):
            uct[pl.ds(k * rc * lb + q, rc, stride=lb), :] = acc[:, q * 128:(q + 1) * 128]
    ubuf[0:8, :] = ubuf[tm:tm + 8, :]

    per = MXU_TILE // 128
    for k in range(c // MXU_TILE):
        cs = slice(k * MXU_TILE, (k + 1) * MXU_TILE)
        lhs = ucb[:, cs]
        rp = jnp.dot(lhs, wrg_ref[k], preferred_element_type=F32) + brg_ref[:, cs]
        ip = jnp.dot(lhs, wig_ref[k], preferred_element_type=F32) + big_ref[:, cs]
        for q in range(per):
            rows = pl.ds(k * per + q, tm, stride=lb)
            rpt[rows, :] = rp[:, q * 128:(q + 1) * 128]
            ipt[rows, :] = ip[:, q * 128:(q + 1) * 128]

    for t, j in enumerate(J_GLRU):
        acc = proj(j)
        glru[:, t * CHUNK:(t + 1) * CHUNK] = acc * _sigmoid(acc)

    z = -lam_ref[...]
    softplus = jnp.maximum(z, 0.0) + jnp.log1p(jnp.exp(-jnp.abs(z)))
    hn = jnp.concatenate([(-0.5 * LRU_C) * softplus] * SCAN_STEPS, axis=0)

    n_scan = tm // SCAN_STEPS
    n_tasks = len(tasks)
    issued = 0
    h = hcar[...]
    for g in range(n_scan):
        if issued < ((g + 1) * n_tasks) // n_scan:
            h = h + jnp.minimum(jnp.abs(tasks.pop(0)()), 0.0)
            issued += 1
        rows = slice(g * SCAN_STEPS * lb, (g + 1) * SCAN_STEPS * lb)
        log_a = jnp.tanh(rpt[rows, :]) * hn + hn
        a = jnp.exp(log_a)
        om = (-1.0 - a * a) * jnp.tanh(log_a)
        sq = jnp.where(om > 0.0, om * lax.rsqrt(om), 0.0)
        ig = 0.5 * jnp.tanh(ipt[rows, :]) + 0.5
        bb = (sq * uct[rows, :]) * ig
        hs = []
        for t in range(SCAN_STEPS):
            h = a[t * lb:(t + 1) * lb, :] * h + bb[t * lb:(t + 1) * lb, :]
            hs.append(h)
        rpt[rows, :] = jnp.concatenate(hs, axis=0)
        rs = slice(g * SCAN_STEPS, (g + 1) * SCAN_STEPS)
        for q in range(lb):
            cs = slice(q * 128, (q + 1) * 128)
            hv = rpt[pl.ds(g * SCAN_STEPS * lb + q, SCAN_STEPS, stride=lb), :]
            hbo_ref[rs, cs] = (hv * glru[rs, cs]).astype(BF16)
    hcar[...] = h
    for task in tasks:
        task()


def _inproj_lru_call(xf, mod_l, g_pre, w_in_b, conv_w, conv_b, wrg_bd, b_rg, wig_bd, b_ig,
                     lam, tiles_per_batch, tm):
    n, d = xf.shape
    wn = w_in_b.shape[1]
    c = conv_w.shape[1]
    rest_w = (len(J_MERGE) + 1) * CHUNK
    nk = c // MXU_TILE
    row = lambda i: (i, 0)
    full2 = lambda i: (0, 0)
    full3 = lambda i: (0, 0, 0)
    return pl.pallas_call(
        functools.partial(_inproj_lru_kernel, tiles_per_batch=tiles_per_batch),
        grid=(n // tm,),
        in_specs=[
            pl.BlockSpec((tm, d), row),
            pl.BlockSpec((1, 3, d), lambda i: (i // tiles_per_batch, 0, 0)),
            pl.BlockSpec((1, d), full2),
            pl.BlockSpec((d, wn), full2, pipeline_mode=pl.Buffered(1)),
            pl.BlockSpec((CONV_W, c), full2),
            pl.BlockSpec((1, c), full2),
            pl.BlockSpec((nk, MXU_TILE, MXU_TILE), full3),
            pl.BlockSpec((1, c), full2),
            pl.BlockSpec((nk, MXU_TILE, MXU_TILE), full3),
            pl.BlockSpec((1, c), full2),
            pl.BlockSpec((c // 128, 128), full2),
        ],
        out_specs=[
            pl.BlockSpec((tm // r, r * 3 * ATT_W), row) for _, r in GROUPS
        ] + [pl.BlockSpec((tm, rest_w), row), pl.BlockSpec((tm, c), row)],
        out_shape=[
            jax.ShapeDtypeStruct((n // r, r * 3 * ATT_W), BF16) for _, r in GROUPS
        ] + [jax.ShapeDtypeStruct((n, rest_w), BF16), jax.ShapeDtypeStruct((n, c), BF16)],
        scratch_shapes=[
            pltpu.VMEM((tm, d), BF16),
            pltpu.VMEM((ATT_W // 128, tm, 128), F32),
            pltpu.VMEM((tm + 8, c), F32),
            pltpu.VMEM((tm, c), BF16),
            pltpu.VMEM((tm * (c // 128), 128), F32),
            pltpu.VMEM((tm * (c // 128), 128), F32),
            pltpu.VMEM((tm * (c // 128), 128), F32),
            pltpu.VMEM((tm, c), F32),
            pltpu.VMEM((c // 128, 128), F32),
        ],
        compiler_params=pltpu.CompilerParams(
            dimension_semantics=("arbitrary",),
            vmem_limit_bytes=VMEM_LIMIT),
        name="inproj_lru",
    )(xf, mod_l, g_pre, w_in_b, conv_w, conv_b, wrg_bd, b_rg, wig_bd, b_ig, lam)


def _attn_kernel(q_ref, kp_ref, kc_ref, vp_ref, vc_ref, o_ref, lse_ref, *, tq):
    n = pl.program_id(2)
    scale = HEAD_DIM ** -0.5
    row = lax.broadcasted_iota(jnp.int32, (BAND, 2 * BAND), 0)
    col = lax.broadcasted_iota(jnp.int32, (BAND, 2 * BAND), 1)
    band = (col >= row) & (col <= row + BAND)
    first = band & ((col >= BAND) | (n > 0))
    lane = lax.broadcasted_iota(jnp.int32, (BAND, 128), 1)
    for i in range(tq // BAND):
        rs = slice(i * BAND, (i + 1) * BAND)
        lses = []
        for h in range(HEADS):
            hs = slice(h * HEAD_DIM, (h + 1) * HEAD_DIM)
            q = q_ref[rs, hs]
            if i == 0:
                kk = jnp.concatenate([kp_ref[:, hs], kc_ref[0:BAND, hs]], axis=0)
                vv = jnp.concatenate([vp_ref[:, hs], vc_ref[0:BAND, hs]], axis=0)
                mask = first
            else:
                kk = kc_ref[(i - 1) * BAND:(i + 1) * BAND, hs]
                vv = vc_ref[(i - 1) * BAND:(i + 1) * BAND, hs]
                mask = band
            s = lax.dot_general(q, kk, (((1,), (1,)), ((), ())),
                                preferred_element_type=F32) * scale
            s = jnp.where(mask, s, NEG)
            m = jnp.max(s, axis=-1, keepdims=True)
            p = jnp.exp(s - m)
            l = jnp.sum(p, axis=-1, keepdims=True)
            o = jnp.dot(p.astype(BF16), vv, preferred_element_type=F32)
            o_ref[rs, hs] = (o * (1.0 / l)).astype(BF16)
            lses.append(jnp.broadcast_to(m + jnp.log(l), (BAND, 128)))
        t = jnp.where(lane < 32, lses[0],
                      jnp.where(lane < 64, lses[1],
                                jnp.where(lane < 96, lses[2], lses[3])))
        lse_ref[rs, :] = t


def _attn_call(a, g, r, batch, seq, tq):
    sub = seq // r
    nblk = sub // tq
    per = tq // BAND

    def cur(kind):
        return lambda b, rho, i: (b * nblk + i, rho * 3 + kind)

    def prev(kind):
        return lambda b, rho, i: (b * nblk * per + jnp.maximum(i * per - 1, 0),
                                  rho * 3 + kind)

    return pl.pallas_call(
        functools.partial(_attn_kernel, tq=tq),
        grid=(batch, r, nblk),
        in_specs=[
            pl.BlockSpec((tq, ATT_W), cur(0)),
            pl.BlockSpec((BAND, ATT_W), prev(1)),
            pl.BlockSpec((tq, ATT_W), cur(1)),
            pl.BlockSpec((BAND, ATT_W), prev(2)),
            pl.BlockSpec((tq, ATT_W), cur(2)),
        ],
        out_specs=[
            pl.BlockSpec((tq, ATT_W), lambda b, rho, i: (b * nblk + i, rho)),
            pl.BlockSpec((tq, 128), lambda b, rho, i: (b * nblk + i, rho)),
        ],
        out_shape=[
            jax.ShapeDtypeStruct((batch * sub, r * ATT_W), BF16),
            jax.ShapeDtypeStruct((batch * sub, r * 128), F32),
        ],
        compiler_params=pltpu.CompilerParams(
            dimension_semantics=("parallel", "parallel", "parallel"),
            vmem_limit_bytes=VMEM_LIMIT),
        name=f"attn_g{g}",
    )(a, a, a, a, a)


def _post_kernel(x_ref, mod_ref, o0_ref, o1_ref, o2_ref, l0_ref, l1_ref, l2_ref,
                 gatt_ref, hb_ref, mrg_ref, wpa_ref, wpb_ref, wo_ref, gpost_ref,
                 out_ref, onat_ref, lnat_ref):
    tm = x_ref.shape[0]
    d = x_ref.shape[1]
    for g, (o_ref, l_ref) in enumerate(((o1_ref, l1_ref), (o2_ref, l2_ref))):
        r = GROUPS[g + 1][1]
        for rho in range(r):
            rows = pl.ds(rho, tm // r, stride=r)
            lnat_ref[g, rows, :] = l_ref[:, rho * 128:(rho + 1) * 128]
            for h in range(HEADS):
                c0 = rho * ATT_W + h * HEAD_DIM
                onat_ref[g, h, rows, :] = o_ref[:, c0:c0 + HEAD_DIM].astype(F32)
    la, lb, lc = l0_ref[...], lnat_ref[0], lnat_ref[1]
    mx = jnp.maximum(jnp.maximum(la, lb), lc)
    ea, eb, ec = jnp.exp(la - mx), jnp.exp(lb - mx), jnp.exp(lc - mx)
    inv = 1.0 / (ea + eb + ec)
    ws = (ea * inv, eb * inv, ec * inv)
    pieces = []
    for h in range(HEADS):
        hs = slice(h * HEAD_DIM, (h + 1) * HEAD_DIM)
        ogs = (o0_ref[:, hs].astype(F32), onat_ref[0, h], onat_ref[1, h])
        acc = None
        for w, og in zip(ws, ogs):
            wb = jnp.broadcast_to(w[:, 32 * h:32 * h + 1], (tm, HEAD_DIM))
            t = wb * og
            acc = t if acc is None else acc + t
        pieces.append((acc * gatt_ref[:, hs].astype(F32)).astype(BF16))
    ab = jnp.concatenate(pieces, axis=1)
    ya = jnp.dot(ab, wpa_ref[...], preferred_element_type=F32)
    yb = jnp.dot(hb_ref[...], wpb_ref[...], preferred_element_type=F32)
    zz = mrg_ref[:, 0:d].astype(F32) * ya + mrg_ref[:, d:2 * d].astype(F32) * yb
    out = jnp.dot(zz.astype(BF16), wo_ref[...], preferred_element_type=F32)
    ms = jnp.mean(out * out, axis=-1, keepdims=True)
    r = out * lax.rsqrt(ms + EPS) * gpost_ref[...]
    out_ref[...] = x_ref[...] + mod_ref[0, 2:3, :] * r


def _post_call(xf, mod_l, os_, lses, rest, hb, wpa_b, wpb_b, wo_b, g_post,
               tiles_per_batch, tm):
    n, d = xf.shape
    row = lambda i: (i, 0)
    gatt_blk = lambda i: (i, (2 * d) // ATT_W)
    full = lambda i: (0, 0)
    return pl.pallas_call(
        _post_kernel,
        grid=(n // tm,),
        in_specs=[
            pl.BlockSpec((tm, d), row),
            pl.BlockSpec((1, 3, d), lambda i: (i // tiles_per_batch, 0, 0)),
        ] + [pl.BlockSpec((tm // r, r * ATT_W), row) for _, r in GROUPS
        ] + [pl.BlockSpec((tm // r, r * 128), row) for _, r in GROUPS
        ] + [
            pl.BlockSpec((tm, ATT_W), gatt_blk),
            pl.BlockSpec((tm, d), row),
            pl.BlockSpec((tm, 2 * d), row),
            pl.BlockSpec((ATT_W, d), full),
            pl.BlockSpec((d, d), full),
            pl.BlockSpec((d, d), full),
            pl.BlockSpec((1, d), full),
        ],
        out_specs=pl.BlockSpec((tm, d), row),
        out_shape=jax.ShapeDtypeStruct((n, d), F32),
        scratch_shapes=[pltpu.VMEM((N_GROUPS - 1, HEADS, tm, HEAD_DIM), F32),
                        pltpu.VMEM((N_GROUPS - 1, tm, 128), F32)],
        compiler_params=pltpu.CompilerParams(
            dimension_semantics=("parallel",),
            vmem_limit_bytes=VMEM_LIMIT),
        name="post",
    )(xf, mod_l, os_[0], os_[1], os_[2], lses[0], lses[1], lses[2],
      rest, hb, rest, wpa_b, wpb_b, wo_b, g_post)


def _block_diag(w):
    nb, bd, _ = w.shape
    per = MXU_TILE // bd
    w4 = w.reshape(nb // per, per, bd, bd)
    eye = jnp.eye(per, dtype=w.dtype)
    return jnp.einsum("cghj,gk->cghkj", w4, eye).reshape(nb // per, MXU_TILE, MXU_TILE)


def kernel(x, c, w_mod, b_mod, g_pre, w_in, conv_w, conv_b, w_rg, b_rg, w_ig, b_ig,
           lru_lambda, w_pa, w_pb, w_o, g_post):
    batch, seq, d = x.shape
    depth = w_mod.shape[0]
    n = batch * seq
    tm_in, tm_post, tq = 512, 256, 512

    mod = _mod_call(c.T, w_mod, b_mod)
    xf = x.reshape(n, d)
    for l in range(depth):
        mod_l = mod[l].reshape(batch, 3, d)
        *a_g, rest, hb = _inproj_lru_call(
            xf, mod_l, g_pre[l][None, :], w_in[l].astype(BF16), conv_w[l], conv_b[l][None, :],
            (0.5 * _block_diag(w_rg[l])).astype(BF16), 0.5 * b_rg[l][None, :],
            (0.5 * _block_diag(w_ig[l])).astype(BF16), 0.5 * b_ig[l][None, :],
            lru_lambda[l].reshape(-1, 128), seq // tm_in, tm_in)
        os_, lses = [], []
        for g, (_, r) in enumerate(GROUPS):
            o, lse = _attn_call(a_g[g], g, r, batch, seq, tq)
            os_.append(o)
            lses.append(lse)
        xf = _post_call(xf, mod_l, os_, lses, rest, hb, w_pa[l].astype(BF16),
                        w_pb[l].astype(BF16), w_o[l].astype(BF16), g_post[l][None, :],
                        seq // tm_post, tm_post)
    return xf.reshape(batch, seq, d)
```

```python
import functools

import jax
import jax.numpy as jnp
from jax import lax
from jax.experimental import pallas as pl
from jax.experimental.pallas import tpu as pltpu

F32 = jnp.float32
BF16 = jnp.bfloat16

HEAD_DIM = 128
HEADS = 4
ATT_W = HEADS * HEAD_DIM
GROUPS = ((128, 1), (512, 4), (2048, 16))
N_GROUPS = 3
BAND = 128
LRU_BLOCKS = 16
CONV_W = 4
LRU_C = 8.0
EPS = 1e-6
NEG = -1e30

QKV_W = 3 * N_GROUPS * ATT_W
CHUNK = 512

VMEM_LIMIT = 58 * 1024 * 1024
MXU_TILE = 256


def _sigmoid(v):
    return 1.0 / (1.0 + jnp.exp(-v))


def _mod_kernel(ct_ref, w_ref, b_ref, o_ref):
    ct = ct_ref[...]
    sc = ct * _sigmoid(ct)
    w = w_ref[0]
    rows = []
    for b in range(ct.shape[1]):
        rows.append(jnp.sum(sc[:, b:b + 1] * w, axis=0, keepdims=True))
    o_ref[0] = jnp.concatenate(rows, axis=0) + b_ref[0]


def _mod_call(ct, w_mod, b_mod):
    depth, d, d3 = w_mod.shape
    nb = ct.shape[1]
    tn = 1024
    return pl.pallas_call(
        _mod_kernel,
        grid=(depth, d3 // tn),
        in_specs=[
            pl.BlockSpec((d, nb), lambda l, j: (0, 0)),
            pl.BlockSpec((1, d, tn), lambda l, j: (l, 0, j)),
            pl.BlockSpec((1, 1, tn), lambda l, j: (l, 0, j)),
        ],
        out_specs=pl.BlockSpec((1, nb, tn), lambda l, j: (l, 0, j)),
        out_shape=jax.ShapeDtypeStruct((depth, nb, d3), F32),
        compiler_params=pltpu.CompilerParams(
            dimension_semantics=("parallel", "parallel"),
            vmem_limit_bytes=VMEM_LIMIT),
        name="mod",
    )(ct, w_mod, b_mod.reshape(depth, 1, d3))


J_GATT = 9
J_U = (10, 11)
J_GLRU = (12, 13)
J_MERGE = (14, 15, 16, 17)
ATTN_TILES = ((1024, 1), (1024, 1), (512, 2))
SCAN_STEPS = 16


def _inproj_lru_kernel(x_ref, mod_ref, gpre_ref, w_ref, cw_ref, cb_ref, wrg_ref, brg_ref,
                       wig_ref, big_ref, lam_ref,
                       a0_ref, a1_ref, a2_ref, rest_ref, hbo_ref,
                       hb_ref, acc_ref, ubuf, ucb, uct, rpt, ipt, glru, hcar,
                       *, tiles_per_batch):
    tm = x_ref.shape[0]
    c = ucb.shape[1]

    @pl.when(pl.program_id(0) % tiles_per_batch == 0)
    def _():
        ubuf[0:8, :] = jnp.zeros((8, c), F32)
        hcar[...] = jnp.zeros(hcar.shape, F32)

    x = x_ref[...]
    ms = jnp.mean(x * x, axis=-1, keepdims=True)
    y = x * lax.rsqrt(ms + EPS) * gpre_ref[...]
    h = y * (1.0 + mod_ref[0, 1:2, :]) + mod_ref[0, 0:1, :]
    hb_ref[...] = h.astype(BF16)

    def proj(j):
        return jnp.dot(hb_ref[...], w_ref[:, j * CHUNK:(j + 1) * CHUNK],
                       preferred_element_type=F32)

    a_refs = (a0_ref, a1_ref, a2_ref)

    def tail(acc):
        return acc[tm - 8:tm, CHUNK - 128:CHUNK]

    def qkv_task(j):
        acc = proj(j)
        kind, g = divmod(j, N_GROUPS)
        r = GROUPS[g][1]
        if r == 1:
            a_refs[g][:, kind * ATT_W:(kind + 1) * ATT_W] = acc.astype(BF16)
        else:
            for sl in range(ATT_W // 128):
                acc_ref[sl] = acc[:, sl * 128:(sl + 1) * 128]
            for rho in range(r):
                c0 = (kind * r + rho) * ATT_W
                for sl in range(ATT_W // 128):
                    v = acc_ref[sl, pl.ds(rho, tm // r, stride=r), :]
                    a_refs[g][:, c0 + sl * 128:c0 + (sl + 1) * 128] = v.astype(BF16)
        return tail(acc)

    def merge_task(t, j):
        acc = proj(j)
        rest_ref[:, t * CHUNK:(t + 1) * CHUNK] = _sigmoid(acc).astype(BF16)
        return tail(acc)

    def gatt_task():
        acc = proj(J_GATT)
        nm = len(J_MERGE)
        rest_ref[:, nm * CHUNK:(nm + 1) * CHUNK] = (acc * _sigmoid(acc)).astype(BF16)
        return tail(acc)

    simple = [functools.partial(merge_task, t, j) for t, j in enumerate(J_MERGE)] + [gatt_task]
    tasks = []
    for kind in range(3):
        for g in (1, 0, 2):
            tasks.append(functools.partial(qkv_task, kind * N_GROUPS + g))
        tasks.append(simple.pop(0))
    tasks += simple


    for t, j in enumerate(J_U):
        ubuf[8:8 + tm, t * CHUNK:(t + 1) * CHUNK] = proj(j)
    for _ in range(2):
        tasks.pop(0)()

    lb = c // 128
    cw = cw_ref[...]
    cb = cb_ref[...]
    rc = 64
    for k in range(tm // rc):
        base = 8 + k * rc
        acc = cb + cw[0:1, :] * ubuf[base:base + rc, :]
        for t in range(1, CONV_W):
            acc = acc + cw[t:t + 1, :] * ubuf[base - t:base - t + rc, :]
        ucb[k * rc:(k + 1) * rc, :] = acc.astype(BF16)
        for q in range(lb):
            uct[pl.ds(k * rc * lb + q, rc, stride=lb), :] = acc[:, q * 128:(q + 1) * 128]
    ubuf[0:8, :] = ubuf[tm:tm + 8, :]

    per = MXU_TILE // 128
    for k in range(c // MXU_TILE):
        cs = slice(k * MXU_TILE, (k + 1) * MXU_TILE)
        lhs = ucb[:, cs]
        rp = jnp.dot(lhs, wrg_ref[k], preferred_element_type=F32) + brg_ref[:, cs]
        ip = jnp.dot(lhs, wig_ref[k], preferred_element_type=F32) + big_ref[:, cs]
        for q in range(per):
            rows = pl.ds(k * per + q, tm, stride=lb)
            rpt[rows, :] = rp[:, q * 128:(q + 1) * 128]
            ipt[rows, :] = ip[:, q * 128:(q + 1) * 128]

    for t, j in enumerate(J_GLRU):
        acc = proj(j)
        glru[:, t * CHUNK:(t + 1) * CHUNK] = acc * _sigmoid(acc)

    z = -lam_ref[...]
    softplus = jnp.maximum(z, 0.0) + jnp.log1p(jnp.exp(-jnp.abs(z)))
    hn = jnp.concatenate([(-0.5 * LRU_C) * softplus] * SCAN_STEPS, axis=0)

    n_scan = tm // SCAN_STEPS
    n_tasks = len(tasks)
    issued = 0
    h = hcar[...]
    for g in range(n_scan):
        if issued < ((g + 1) * n_tasks) // n_scan:
            h = h + jnp.minimum(jnp.abs(tasks.pop(0)()), 0.0)
            issued += 1
        rows = slice(g * SCAN_STEPS * lb, (g + 1) * SCAN_STEPS * lb)
        log_a = jnp.tanh(rpt[rows, :]) * hn + hn
        a = jnp.exp(log_a)
        om = (-1.0 - a * a) * jnp.tanh(log_a)
        sq = jnp.where(om > 0.0, om * lax.rsqrt(om), 0.0)
        ig = 0.5 * jnp.tanh(ipt[rows, :]) + 0.5
        bb = (sq * uct[rows, :]) * ig
        hs = []
        for t in range(SCAN_STEPS):
            h = a[t * lb:(t + 1) * lb, :] * h + bb[t * lb:(t + 1) * lb, :]
            hs.append(h)
        rpt[rows, :] = jnp.concatenate(hs, axis=0)
        rs = slice(g * SCAN_STEPS, (g + 1) * SCAN_STEPS)
        for q in range(lb):
            cs = slice(q * 128, (q + 1) * 128)
            hv = rpt[pl.ds(g * SCAN_STEPS * lb + q, SCAN_STEPS, stride=lb), :]
            hbo_ref[rs, cs] = (hv * glru[rs, cs]).astype(BF16)
    hcar[...] = h
    for task in tasks:
        task()


def _inproj_lru_call(xf, mod_l, g_pre, w_in_b, conv_w, conv_b, wrg_bd, b_rg, wig_bd, b_ig,
                     lam, tiles_per_batch, tm):
    n, d = xf.shape
    wn = w_in_b.shape[1]
    c = conv_w.shape[1]
    rest_w = (len(J_MERGE) + 1) * CHUNK
    nk = c // MXU_TILE
    row = lambda i: (i, 0)
    full2 = lambda i: (0, 0)
    full3 = lambda i: (0, 0, 0)
    return pl.pallas_call(
        functools.partial(_inproj_lru_kernel, tiles_per_batch=tiles_per_batch),
        grid=(n // tm,),
        in_specs=[
            pl.BlockSpec((tm, d), row),
            pl.BlockSpec((1, 3, d), lambda i: (i // tiles_per_batch, 0, 0)),
            pl.BlockSpec((1, d), full2),
            pl.BlockSpec((d, wn), full2, pipeline_mode=pl.Buffered(1)),
            pl.BlockSpec((CONV_W, c), full2),
            pl.BlockSpec((1, c), full2),
            pl.BlockSpec((nk, MXU_TILE, MXU_TILE), full3),
            pl.BlockSpec((1, c), full2),
            pl.BlockSpec((nk, MXU_TILE, MXU_TILE), full3),
            pl.BlockSpec((1, c), full2),
            pl.BlockSpec((c // 128, 128), full2),
        ],
        out_specs=[
            pl.BlockSpec((tm // r, r * 3 * ATT_W), row) for _, r in GROUPS
        ] + [pl.BlockSpec((tm, rest_w), row), pl.BlockSpec((tm, c), row)],
        out_shape=[
            jax.ShapeDtypeStruct((n // r, r * 3 * ATT_W), BF16) for _, r in GROUPS
        ] + [jax.ShapeDtypeStruct((n, rest_w), BF16), jax.ShapeDtypeStruct((n, c), BF16)],
        scratch_shapes=[
            pltpu.VMEM((tm, d), BF16),
            pltpu.VMEM((ATT_W // 128, tm, 128), F32),
            pltpu.VMEM((tm + 8, c), F32),
            pltpu.VMEM((tm, c), BF16),
            pltpu.VMEM((tm * (c // 128), 128), F32),
            pltpu.VMEM((tm * (c // 128), 128), F32),
            pltpu.VMEM((tm * (c // 128), 128), F32),
            pltpu.VMEM((tm, c), F32),
            pltpu.VMEM((c // 128, 128), F32),
        ],
        compiler_params=pltpu.CompilerParams(
            dimension_semantics=("arbitrary",),
            vmem_limit_bytes=VMEM_LIMIT),
        name="inproj_lru",
    )(xf, mod_l, g_pre, w_in_b, conv_w, conv_b, wrg_bd, b_rg, wig_bd, b_ig, lam)


def _attn_kernel(q_ref, kp_ref, kc_ref, vp_ref, vc_ref, o_ref, m_ref, l_ref, *, tq, nres):
    n = pl.program_id(2)
    scale = HEAD_DIM ** -0.5
    c2 = scale * 1.4426950408889634
    row = lax.broadcasted_iota(jnp.int32, (BAND, 2 * BAND), 0)
    col = lax.broadcasted_iota(jnp.int32, (BAND, 2 * BAND), 1)
    band = (col >= row) & (col <= row + BAND)
    first = band & ((col >= BAND) | (n > 0))
    lane = lax.broadcasted_iota(jnp.int32, (BAND, 128), 1)

    def by_head(vals):
        return jnp.where(lane < 32, vals[0],
                         jnp.where(lane < 64, vals[1],
                                   jnp.where(lane < 96, vals[2], vals[3])))

    for rr in range(nres):
        for i in range(tq // BAND):
            rs = slice(i * BAND, (i + 1) * BAND)
            ms, ls = [], []
            for h in range(HEADS):
                hs = slice(rr * ATT_W + h * HEAD_DIM, rr * ATT_W + (h + 1) * HEAD_DIM)
                q = q_ref[rs, hs]
                if i == 0:
                    kk = jnp.concatenate([kp_ref[:, hs], kc_ref[0:BAND, hs]], axis=0)
                    vv = jnp.concatenate([vp_ref[:, hs], vc_ref[0:BAND, hs]], axis=0)
                    mask = first
                else:
                    kk = kc_ref[(i - 1) * BAND:(i + 1) * BAND, hs]
                    vv = vc_ref[(i - 1) * BAND:(i + 1) * BAND, hs]
                    mask = band
                s = lax.dot_general(q, kk, (((1,), (1,)), ((), ())),
                                    preferred_element_type=F32)
                s = jnp.where(mask, s, NEG)
                m = jnp.max(s, axis=-1, keepdims=True)
                p = jnp.exp2((s - m) * c2)
                l = jnp.sum(p, axis=-1, keepdims=True)
                o = jnp.dot(p.astype(BF16), vv, preferred_element_type=F32)
                o_ref[rs, hs] = o.astype(BF16)
                ms.append(jnp.broadcast_to(m * scale, (BAND, 128)))
                ls.append(jnp.broadcast_to(l, (BAND, 128)))
            m_ref[rs, rr * 128:(rr + 1) * 128] = by_head(ms)
            l_ref[rs, rr * 128:(rr + 1) * 128] = by_head(ls)


def _attn_call(a, g, r, batch, seq, tq, nres):
    sub = seq // r
    nblk = sub // tq
    per = tq // BAND
    nrb = r // nres

    def cur(kind):
        return lambda b, rb, i: (b * nblk + i, kind * nrb + rb)

    def prev(kind):
        return lambda b, rb, i: (b * nblk * per + jnp.maximum(i * per - 1, 0),
                                 kind * nrb + rb)

    out_map = lambda b, rb, i: (b * nblk + i, rb)
    return pl.pallas_call(
        functools.partial(_attn_kernel, tq=tq, nres=nres),
        grid=(batch, nrb, nblk),
        in_specs=[
            pl.BlockSpec((tq, nres * ATT_W), cur(0)),
            pl.BlockSpec((BAND, nres * ATT_W), prev(1)),
            pl.BlockSpec((tq, nres * ATT_W), cur(1)),
            pl.BlockSpec((BAND, nres * ATT_W), prev(2)),
            pl.BlockSpec((tq, nres * ATT_W), cur(2)),
        ],
        out_specs=[
            pl.BlockSpec((tq, nres * ATT_W), out_map),
            pl.BlockSpec((tq, nres * 128), out_map),
            pl.BlockSpec((tq, nres * 128), out_map),
        ],
        out_shape=[
            jax.ShapeDtypeStruct((batch * sub, r * ATT_W), BF16),
            jax.ShapeDtypeStruct((batch * sub, r * 128), F32),
            jax.ShapeDtypeStruct((batch * sub, r * 128), F32),
        ],
        compiler_params=pltpu.CompilerParams(
            dimension_semantics=("parallel", "parallel", "parallel"),
            vmem_limit_bytes=VMEM_LIMIT),
        name=f"attn_g{g}",
    )(a, a, a, a, a)


def _post_kernel(x_ref, mod_ref, o0_ref, o1_ref, o2_ref, m0_ref, m1_ref, m2_ref,
                 l0_ref, l1_ref, l2_ref, gatt_ref, hb_ref, mrg_ref, wpa_ref, wpb_ref, wo_ref,
                 gpost_ref, out_ref, onat_ref, snat_ref):
    tm = x_ref.shape[0]
    d = x_ref.shape[1]
    for g, (o_ref, m_ref, l_ref) in enumerate(((o1_ref, m1_ref, l1_ref),
                                               (o2_ref, m2_ref, l2_ref))):
        r = GROUPS[g + 1][1]
        for rho in range(r):
            rows = pl.ds(rho, tm // r, stride=r)
            snat_ref[g, 0, rows, :] = m_ref[:, rho * 128:(rho + 1) * 128]
            snat_ref[g, 1, rows, :] = l_ref[:, rho * 128:(rho + 1) * 128]
            for h in range(HEADS):
                c0 = rho * ATT_W + h * HEAD_DIM
                onat_ref[g, h, rows, :] = o_ref[:, c0:c0 + HEAD_DIM].astype(F32)
    ma, mb, mc = m0_ref[...], snat_ref[0, 0], snat_ref[1, 0]
    mx = jnp.maximum(jnp.maximum(ma, mb), mc)
    ea, eb, ec = jnp.exp(ma - mx), jnp.exp(mb - mx), jnp.exp(mc - mx)
    inv = 1.0 / (ea * l0_ref[...] + eb * snat_ref[0, 1] + ec * snat_ref[1, 1])
    ws = (ea * inv, eb * inv, ec * inv)
    pieces = []
    for h in range(HEADS):
        hs = slice(h * HEAD_DIM, (h + 1) * HEAD_DIM)
        ogs = (o0_ref[:, hs].astype(F32), onat_ref[0, h], onat_ref[1, h])
        acc = None
        for w, og in zip(ws, ogs):
            wb = jnp.broadcast_to(w[:, 32 * h:32 * h + 1], (tm, HEAD_DIM))
            t = wb * og
            acc = t if acc is None else acc + t
        pieces.append((acc * gatt_ref[:, hs].astype(F32)).astype(BF16))
    ab = jnp.concatenate(pieces, axis=1)
    ya = jnp.dot(ab, wpa_ref[...], preferred_element_type=F32)
    yb = jnp.dot(hb_ref[...], wpb_ref[...], preferred_element_type=F32)
    zz = mrg_ref[:, 0:d].astype(F32) * ya + mrg_ref[:, d:2 * d].astype(F32) * yb
    out = jnp.dot(zz.astype(BF16), wo_ref[...], preferred_element_type=F32)
    ms = jnp.mean(out * out, axis=-1, keepdims=True)
    r = out * lax.rsqrt(ms + EPS) * gpost_ref[...]
    out_ref[...] = x_ref[...] + mod_ref[0, 2:3, :] * r


def _post_call(xf, mod_l, os_, ms_, ls_, rest, hb, wpa_b, wpb_b, wo_b, g_post,
               tiles_per_batch, tm):
    n, d = xf.shape
    row = lambda i: (i, 0)
    gatt_blk = lambda i: (i, (2 * d) // ATT_W)
    full = lambda i: (0, 0)
    return pl.pallas_call(
        _post_kernel,
        grid=(n // tm,),
        in_specs=[
            pl.BlockSpec((tm, d), row),
            pl.BlockSpec((1, 3, d), lambda i: (i // tiles_per_batch, 0, 0)),
        ] + [pl.BlockSpec((tm // r, r * ATT_W), row) for _, r in GROUPS
        ] + [pl.BlockSpec((tm // r, r * 128), row) for _, r in GROUPS
        ] + [pl.BlockSpec((tm // r, r * 128), row) for _, r in GROUPS
        ] + [
            pl.BlockSpec((tm, ATT_W), gatt_blk),
            pl.BlockSpec((tm, d), row),
            pl.BlockSpec((tm, 2 * d), row),
            pl.BlockSpec((ATT_W, d), full),
            pl.BlockSpec((d, d), full),
            pl.BlockSpec((d, d), full),
            pl.BlockSpec((1, d), full),
        ],
        out_specs=pl.BlockSpec((tm, d), row),
        out_shape=jax.ShapeDtypeStruct((n, d), F32),
        scratch_shapes=[pltpu.VMEM((N_GROUPS - 1, HEADS, tm, HEAD_DIM), F32),
                        pltpu.VMEM((N_GROUPS - 1, 2, tm, 128), F32)],
        compiler_params=pltpu.CompilerParams(
            dimension_semantics=("parallel",),
            vmem_limit_bytes=VMEM_LIMIT),
        name="post",
    )(xf, mod_l, *os_, *ms_, *ls_, rest, hb, rest, wpa_b, wpb_b, wo_b, g_post)


def _block_diag(w):
    nb, bd, _ = w.shape
    per = MXU_TILE // bd
    w4 = w.reshape(nb // per, per, bd, bd)
    eye = jnp.eye(per, dtype=w.dtype)
    return jnp.einsum("cghj,gk->cghkj", w4, eye).reshape(nb // per, MXU_TILE, MXU_TILE)


def kernel(x, c, w_mod, b_mod, g_pre, w_in, conv_w, conv_b, w_rg, b_rg, w_ig, b_ig,
           lru_lambda, w_pa, w_pb, w_o, g_post):
    batch, seq, d = x.shape
    depth = w_mod.shape[0]
    n = batch * seq
    tm_in, tm_post = 512, 256

    mod = _mod_call(c.T, w_mod, b_mod)
    xf = x.reshape(n, d)
    for l in range(depth):
        mod_l = mod[l].reshape(batch, 3, d)
        *a_g, rest, hb = _inproj_lru_call(
            xf, mod_l, g_pre[l][None, :], w_in[l].astype(BF16), conv_w[l], conv_b[l][None, :],
            (0.5 * _block_diag(w_rg[l])).astype(BF16), 0.5 * b_rg[l][None, :],
            (0.5 * _block_diag(w_ig[l])).astype(BF16), 0.5 * b_ig[l][None, :],
            lru_lambda[l].reshape(-1, 128), seq // tm_in, tm_in)
        os_, ms_, ls_ = [], [], []
        for g, (_, r) in enumerate(GROUPS):
            o, mx, den = _attn_call(a_g[g], g, r, batch, seq, *ATTN_TILES[g])
            os_.append(o)
            ms_.append(mx)
            ls_.append(den)
        xf = _post_call(xf, mod_l, os_, ms_, ls_, rest, hb, w_pa[l].astype(BF16),
                        w_pb[l].astype(BF16), w_o[l].astype(BF16), g_post[l][None, :],
                        seq // tm_post, tm_post)
    return xf.reshape(batch, seq, d)
```

```python
import functools

import jax
import jax.numpy as jnp
from jax import lax
from jax.experimental import pallas as pl
from jax.experimental.pallas import tpu as pltpu

F32 = jnp.float32
BF16 = jnp.bfloat16

HEAD_DIM = 128
HEADS = 4
ATT_W = HEADS * HEAD_DIM
GROUPS = ((128, 1), (512, 4), (2048, 16))
N_GROUPS = 3
BAND = 128
LRU_BLOCKS = 16
CONV_W = 4
LRU_C = 8.0
EPS = 1e-6
NEG = -1e30

QKV_W = 3 * N_GROUPS * ATT_W
CHUNK = 512

VMEM_LIMIT = 58 * 1024 * 1024
MXU_TILE = 256


def _sigmoid(v):
    return 1.0 / (1.0 + jnp.exp(-v))


def _mod_kernel(ct_ref, w_ref, b_ref, o_ref):
    ct = ct_ref[...]
    sc = ct * _sigmoid(ct)
    w = w_ref[0]
    rows = []
    for b in range(ct.shape[1]):
        rows.append(jnp.sum(sc[:, b:b + 1] * w, axis=0, keepdims=True))
    o_ref[0] = jnp.concatenate(rows, axis=0) + b_ref[0]


def _mod_call(ct, w_mod, b_mod):
    depth, d, d3 = w_mod.shape
    nb = ct.shape[1]
    tn = 1024
    return pl.pallas_call(
        _mod_kernel,
        grid=(depth, d3 // tn),
        in_specs=[
            pl.BlockSpec((d, nb), lambda l, j: (0, 0)),
            pl.BlockSpec((1, d, tn), lambda l, j: (l, 0, j)),
            pl.BlockSpec((1, 1, tn), lambda l, j: (l, 0, j)),
        ],
        out_specs=pl.BlockSpec((1, nb, tn), lambda l, j: (l, 0, j)),
        out_shape=jax.ShapeDtypeStruct((depth, nb, d3), F32),
        compiler_params=pltpu.CompilerParams(
            dimension_semantics=("parallel", "parallel"),
            vmem_limit_bytes=VMEM_LIMIT),
        name="mod",
    )(ct, w_mod, b_mod.reshape(depth, 1, d3))


J_GATT = 9
J_U = (10, 11)
J_GLRU = (12, 13)
J_MERGE = (14, 15, 16, 17)
ATTN_TILES = ((1024, 1), (1024, 1), (512, 2))
SCAN_STEPS = 16
POST_SUB = 256


def _inproj_lru_kernel(x_ref, mod_ref, gpre_ref, w_ref, cw_ref, cb_ref, wrg_ref, brg_ref,
                       wig_ref, big_ref, lam_ref,
                       a0_ref, a1_ref, a2_ref, rest_ref, hbo_ref,
                       hb_ref, acc_ref, ubuf, ucb, uct, rpt, ipt, glru, hcar,
                       *, tiles_per_batch):
    tm = x_ref.shape[0]
    c = ucb.shape[1]

    @pl.when(pl.program_id(0) % tiles_per_batch == 0)
    def _():
        ubuf[0:8, :] = jnp.zeros((8, c), F32)
        hcar[...] = jnp.zeros(hcar.shape, F32)

    x = x_ref[...]
    ms = jnp.mean(x * x, axis=-1, keepdims=True)
    y = x * lax.rsqrt(ms + EPS) * gpre_ref[...]
    h = y * (1.0 + mod_ref[0, 1:2, :]) + mod_ref[0, 0:1, :]
    hb_ref[...] = h.astype(BF16)

    def proj(j):
        return jnp.dot(hb_ref[...], w_ref[:, j * CHUNK:(j + 1) * CHUNK],
                       preferred_element_type=F32)

    a_refs = (a0_ref, a1_ref, a2_ref)

    def tail(acc):
        return acc[tm - 8:tm, CHUNK - 128:CHUNK]

    def qkv_task(j):
        acc = proj(j)
        kind, g = divmod(j, N_GROUPS)
        r = GROUPS[g][1]
        if r == 1:
            a_refs[g][:, kind * ATT_W:(kind + 1) * ATT_W] = acc.astype(BF16)
        else:
            for sl in range(ATT_W // 128):
                acc_ref[sl] = acc[:, sl * 128:(sl + 1) * 128]
            for rho in range(r):
                c0 = (kind * r + rho) * ATT_W
                for sl in range(ATT_W // 128):
                    v = acc_ref[sl, pl.ds(rho, tm // r, stride=r), :]
                    a_refs[g][:, c0 + sl * 128:c0 + (sl + 1) * 128] = v.astype(BF16)
        return tail(acc)

    def merge_task(t, j):
        acc = proj(j)
        rest_ref[:, t * CHUNK:(t + 1) * CHUNK] = _sigmoid(acc).astype(BF16)
        return tail(acc)

    def gatt_task():
        acc = proj(J_GATT)
        nm = len(J_MERGE)
        rest_ref[:, nm * CHUNK:(nm + 1) * CHUNK] = (acc * _sigmoid(acc)).astype(BF16)
        return tail(acc)

    simple = [functools.partial(merge_task, t, j) for t, j in enumerate(J_MERGE)] + [gatt_task]
    tasks = []
    for kind in range(3):
        for g in (1, 0, 2):
            tasks.append(functools.partial(qkv_task, kind * N_GROUPS + g))
        tasks.append(simple.pop(0))
    tasks += simple


    for t, j in enumerate(J_U):
        ubuf[8:8 + tm, t * CHUNK:(t + 1) * CHUNK] = proj(j)
    for _ in range(2):
        tasks.pop(0)()

    lb = c // 128
    cw = cw_ref[...]
    cb = cb_ref[...]
    rc = 64
    for k in range(tm // rc):
        base = 8 + k * rc
        acc = cb + cw[0:1, :] * ubuf[base:base + rc, :]
        for t in range(1, CONV_W):
            acc = acc + cw[t:t + 1, :] * ubuf[base - t:base - t + rc, :]
        ucb[k * rc:(k + 1) * rc, :] = acc.astype(BF16)
        for q in range(lb):
            uct[pl.ds(k * rc * lb + q, rc, stride=lb), :] = acc[:, q * 128:(q + 1) * 128]
    ubuf[0:8, :] = ubuf[tm:tm + 8, :]

    per = MXU_TILE // 128
    for k in range(c // MXU_TILE):
        cs = slice(k * MXU_TILE, (k + 1) * MXU_TILE)
        lhs = ucb[:, cs]
        rp = jnp.dot(lhs, wrg_ref[k], preferred_element_type=F32) + brg_ref[:, cs]
        ip = jnp.dot(lhs, wig_ref[k], preferred_element_type=F32) + big_ref[:, cs]
        for q in range(per):
            rows = pl.ds(k * per + q, tm, stride=lb)
            rpt[rows, :] = rp[:, q * 128:(q + 1) * 128]
            ipt[rows, :] = ip[:, q * 128:(q + 1) * 128]

    for t, j in enumerate(J_GLRU):
        acc = proj(j)
        glru[:, t * CHUNK:(t + 1) * CHUNK] = acc * _sigmoid(acc)

    z = -lam_ref[...]
    softplus = jnp.maximum(z, 0.0) + jnp.log1p(jnp.exp(-jnp.abs(z)))
    hn = jnp.concatenate([(-0.5 * LRU_C) * softplus] * SCAN_STEPS, axis=0)

    n_scan = tm // SCAN_STEPS
    n_tasks = len(tasks)
    issued = 0
    h = hcar[...]
    for g in range(n_scan):
        if issued < ((g + 1) * n_tasks) // n_scan:
            h = h + jnp.minimum(jnp.abs(tasks.pop(0)()), 0.0)
            issued += 1
        rows = slice(g * SCAN_STEPS * lb, (g + 1) * SCAN_STEPS * lb)
        log_a = jnp.tanh(rpt[rows, :]) * hn + hn
        a = jnp.exp(log_a)
        om = (-1.0 - a * a) * jnp.tanh(log_a)
        sq = jnp.where(om > 0.0, om * lax.rsqrt(om), 0.0)
        ig = 0.5 * jnp.tanh(ipt[rows, :]) + 0.5
        bb = (sq * uct[rows, :]) * ig
        hs = []
        for t in range(SCAN_STEPS):
            h = a[t * lb:(t + 1) * lb, :] * h + bb[t * lb:(t + 1) * lb, :]
            hs.append(h)
        rpt[rows, :] = jnp.concatenate(hs, axis=0)
        rs = slice(g * SCAN_STEPS, (g + 1) * SCAN_STEPS)
        for q in range(lb):
            cs = slice(q * 128, (q + 1) * 128)
            hv = rpt[pl.ds(g * SCAN_STEPS * lb + q, SCAN_STEPS, stride=lb), :]
            hbo_ref[rs, cs] = (hv * glru[rs, cs]).astype(BF16)
    hcar[...] = h
    for task in tasks:
        task()


def _inproj_lru_call(xf, mod_l, g_pre, w_in_b, conv_w, conv_b, wrg_bd, b_rg, wig_bd, b_ig,
                     lam, tiles_per_batch, tm):
    n, d = xf.shape
    wn = w_in_b.shape[1]
    c = conv_w.shape[1]
    rest_w = (len(J_MERGE) + 1) * CHUNK
    nk = c // MXU_TILE
    row = lambda i: (i, 0)
    full2 = lambda i: (0, 0)
    full3 = lambda i: (0, 0, 0)
    return pl.pallas_call(
        functools.partial(_inproj_lru_kernel, tiles_per_batch=tiles_per_batch),
        grid=(n // tm,),
        in_specs=[
            pl.BlockSpec((tm, d), row),
            pl.BlockSpec((1, 3, d), lambda i: (i // tiles_per_batch, 0, 0)),
            pl.BlockSpec((1, d), full2),
            pl.BlockSpec((d, wn), full2, pipeline_mode=pl.Buffered(1)),
            pl.BlockSpec((CONV_W, c), full2),
            pl.BlockSpec((1, c), full2),
            pl.BlockSpec((nk, MXU_TILE, MXU_TILE), full3),
            pl.BlockSpec((1, c), full2),
            pl.BlockSpec((nk, MXU_TILE, MXU_TILE), full3),
            pl.BlockSpec((1, c), full2),
            pl.BlockSpec((c // 128, 128), full2),
        ],
        out_specs=[
            pl.BlockSpec((tm // r, r * 3 * ATT_W), row) for _, r in GROUPS
        ] + [pl.BlockSpec((tm, rest_w), row), pl.BlockSpec((tm, c), row)],
        out_shape=[
            jax.ShapeDtypeStruct((n // r, r * 3 * ATT_W), BF16) for _, r in GROUPS
        ] + [jax.ShapeDtypeStruct((n, rest_w), BF16), jax.ShapeDtypeStruct((n, c), BF16)],
        scratch_shapes=[
            pltpu.VMEM((tm, d), BF16),
            pltpu.VMEM((ATT_W // 128, tm, 128), F32),
            pltpu.VMEM((tm + 8, c), F32),
            pltpu.VMEM((tm, c), BF16),
            pltpu.VMEM((tm * (c // 128), 128), F32),
            pltpu.VMEM((tm * (c // 128), 128), F32),
            pltpu.VMEM((tm * (c // 128), 128), F32),
            pltpu.VMEM((tm, c), F32),
            pltpu.VMEM((c // 128, 128), F32),
        ],
        compiler_params=pltpu.CompilerParams(
            dimension_semantics=("arbitrary",),
            vmem_limit_bytes=VMEM_LIMIT),
        name="inproj_lru",
    )(xf, mod_l, g_pre, w_in_b, conv_w, conv_b, wrg_bd, b_rg, wig_bd, b_ig, lam)


def _attn_kernel(q_ref, kp_ref, kc_ref, vp_ref, vc_ref, o_ref, m_ref, l_ref, *, tq, nres):
    n = pl.program_id(2)
    scale = HEAD_DIM ** -0.5
    c2 = scale * 1.4426950408889634
    row = lax.broadcasted_iota(jnp.int32, (BAND, 2 * BAND), 0)
    col = lax.broadcasted_iota(jnp.int32, (BAND, 2 * BAND), 1)
    band = (col >= row) & (col <= row + BAND)
    first = band & ((col >= BAND) | (n > 0))
    lane = lax.broadcasted_iota(jnp.int32, (BAND, 128), 1)

    def by_head(vals):
        return jnp.where(lane < 32, vals[0],
                         jnp.where(lane < 64, vals[1],
                                   jnp.where(lane < 96, vals[2], vals[3])))

    for rr in range(nres):
        for i in range(tq // BAND):
            rs = slice(i * BAND, (i + 1) * BAND)
            ms, ls = [], []
            for h in range(HEADS):
                hs = slice(rr * ATT_W + h * HEAD_DIM, rr * ATT_W + (h + 1) * HEAD_DIM)
                q = q_ref[rs, hs]
                if i == 0:
                    kk = jnp.concatenate([kp_ref[:, hs], kc_ref[0:BAND, hs]], axis=0)
                    vv = jnp.concatenate([vp_ref[:, hs], vc_ref[0:BAND, hs]], axis=0)
                    mask = first
                else:
                    kk = kc_ref[(i - 1) * BAND:(i + 1) * BAND, hs]
                    vv = vc_ref[(i - 1) * BAND:(i + 1) * BAND, hs]
                    mask = band
                s = lax.dot_general(q, kk, (((1,), (1,)), ((), ())),
                                    preferred_element_type=F32)
                s = jnp.where(mask, s, NEG)
                m = jnp.max(s, axis=-1, keepdims=True)
                p = jnp.exp2((s - m) * c2)
                l = jnp.sum(p, axis=-1, keepdims=True)
                o = jnp.dot(p.astype(BF16), vv, preferred_element_type=F32)
                o_ref[rs, hs] = o.astype(BF16)
                ms.append(jnp.broadcast_to(m * scale, (BAND, 128)))
                ls.append(jnp.broadcast_to(l, (BAND, 128)))
            m_ref[rs, rr * 128:(rr + 1) * 128] = by_head(ms)
            l_ref[rs, rr * 128:(rr + 1) * 128] = by_head(ls)


def _attn_call(a, g, r, batch, seq, tq, nres):
    sub = seq // r
    nblk = sub // tq
    per = tq // BAND
    nrb = r // nres

    def cur(kind):
        return lambda b, rb, i: (b * nblk + i, kind * nrb + rb)

    def prev(kind):
        return lambda b, rb, i: (b * nblk * per + jnp.maximum(i * per - 1, 0),
                                 kind * nrb + rb)

    out_map = lambda b, rb, i: (b * nblk + i, rb)
    return pl.pallas_call(
        functools.partial(_attn_kernel, tq=tq, nres=nres),
        grid=(batch, nrb, nblk),
        in_specs=[
            pl.BlockSpec((tq, nres * ATT_W), cur(0)),
            pl.BlockSpec((BAND, nres * ATT_W), prev(1)),
            pl.BlockSpec((tq, nres * ATT_W), cur(1)),
            pl.BlockSpec((BAND, nres * ATT_W), prev(2)),
            pl.BlockSpec((tq, nres * ATT_W), cur(2)),
        ],
        out_specs=[
            pl.BlockSpec((tq, nres * ATT_W), out_map),
            pl.BlockSpec((tq, nres * 128), out_map),
            pl.BlockSpec((tq, nres * 128), out_map),
        ],
        out_shape=[
            jax.ShapeDtypeStruct((batch * sub, r * ATT_W), BF16),
            jax.ShapeDtypeStruct((batch * sub, r * 128), F32),
            jax.ShapeDtypeStruct((batch * sub, r * 128), F32),
        ],
        compiler_params=pltpu.CompilerParams(
            dimension_semantics=("parallel", "parallel", "parallel"),
            vmem_limit_bytes=VMEM_LIMIT),
        name=f"attn_g{g}",
    )(a, a, a, a, a)


def _post_kernel(x_ref, mod_ref, o0_ref, o1_ref, o2_ref, m0_ref, m1_ref, m2_ref,
                 l0_ref, l1_ref, l2_ref, gatt_ref, hb_ref, mrg_ref, wpa_ref, wpb_ref, wo_ref,
                 gpost_ref, out_ref, onat_ref, snat_ref):
    tm = x_ref.shape[0]
    d = x_ref.shape[1]
    for s0 in range(0, tm, POST_SUB):
        rs = slice(s0, s0 + POST_SUB)
        for g, (o_ref, m_ref, l_ref) in enumerate(((o1_ref, m1_ref, l1_ref),
                                                   (o2_ref, m2_ref, l2_ref))):
            r = GROUPS[g + 1][1]
            ls = slice(s0 // r, (s0 + POST_SUB) // r)
            for rho in range(r):
                rows = pl.ds(s0 + rho, POST_SUB // r, stride=r)
                snat_ref[g, 0, rows, :] = m_ref[ls, rho * 128:(rho + 1) * 128]
                snat_ref[g, 1, rows, :] = l_ref[ls, rho * 128:(rho + 1) * 128]
                for h in range(HEADS):
                    c0 = rho * ATT_W + h * HEAD_DIM
                    onat_ref[g, h, rows, :] = o_ref[ls, c0:c0 + HEAD_DIM].astype(F32)
        ma, mb, mc = m0_ref[rs, :], snat_ref[0, 0, rs, :], snat_ref[1, 0, rs, :]
        mx = jnp.maximum(jnp.maximum(ma, mb), mc)
        ea, eb, ec = jnp.exp(ma - mx), jnp.exp(mb - mx), jnp.exp(mc - mx)
        inv = 1.0 / (ea * l0_ref[rs, :] + eb * snat_ref[0, 1, rs, :] + ec * snat_ref[1, 1, rs, :])
        ws = (ea * inv, eb * inv, ec * inv)
        pieces = []
        for h in range(HEADS):
            hs = slice(h * HEAD_DIM, (h + 1) * HEAD_DIM)
            ogs = (o0_ref[rs, hs].astype(F32), onat_ref[0, h, rs, :], onat_ref[1, h, rs, :])
            acc = None
            for w, og in zip(ws, ogs):
                wb = jnp.broadcast_to(w[:, 32 * h:32 * h + 1], (POST_SUB, HEAD_DIM))
                t = wb * og
                acc = t if acc is None else acc + t
            pieces.append((acc * gatt_ref[rs, hs].astype(F32)).astype(BF16))
        ab = jnp.concatenate(pieces, axis=1)
        ya = jnp.dot(ab, wpa_ref[...], preferred_element_type=F32)
        yb = jnp.dot(hb_ref[rs, :], wpb_ref[...], preferred_element_type=F32)
        zz = mrg_ref[rs, 0:d].astype(F32) * ya + mrg_ref[rs, d:2 * d].astype(F32) * yb
        out = jnp.dot(zz.astype(BF16), wo_ref[...], preferred_element_type=F32)
        ms = jnp.mean(out * out, axis=-1, keepdims=True)
        r = out * lax.rsqrt(ms + EPS) * gpost_ref[...]
        out_ref[rs, :] = x_ref[rs, :] + mod_ref[0, 2:3, :] * r


def _post_call(xf, mod_l, os_, ms_, ls_, rest, hb, wpa_b, wpb_b, wo_b, g_post,
               tiles_per_batch, tm):
    n, d = xf.shape
    row = lambda i: (i, 0)
    gatt_blk = lambda i: (i, (2 * d) // ATT_W)
    full = lambda i: (0, 0)
    return pl.pallas_call(
        _post_kernel,
        grid=(n // tm,),
        in_specs=[
            pl.BlockSpec((tm, d), row),
            pl.BlockSpec((1, 3, d), lambda i: (i // tiles_per_batch, 0, 0)),
        ] + [pl.BlockSpec((tm // r, r * ATT_W), row) for _, r in GROUPS
        ] + [pl.BlockSpec((tm // r, r * 128), row) for _, r in GROUPS
        ] + [pl.BlockSpec((tm // r, r * 128), row) for _, r in GROUPS
        ] + [
            pl.BlockSpec((tm, ATT_W), gatt_blk),
            pl.BlockSpec((tm, d), row),
            pl.BlockSpec((tm, 2 * d), row),
            pl.BlockSpec((ATT_W, d), full),
            pl.BlockSpec((d, d), full),
            pl.BlockSpec((d, d), full),
            pl.BlockSpec((1, d), full),
        ],
        out_specs=pl.BlockSpec((tm, d), row),
        out_shape=jax.ShapeDtypeStruct((n, d), F32),
        scratch_shapes=[pltpu.VMEM((N_GROUPS - 1, HEADS, tm, HEAD_DIM), F32),
                        pltpu.VMEM((N_GROUPS - 1, 2, tm, 128), F32)],
        compiler_params=pltpu.CompilerParams(
            dimension_semantics=("parallel",),
            vmem_limit_bytes=VMEM_LIMIT),
        name="post",
    )(xf, mod_l, *os_, *ms_, *ls_, rest, hb, rest, wpa_b, wpb_b, wo_b, g_post)


def _block_diag(w):
    nb, bd, _ = w.shape
    per = MXU_TILE // bd
    w4 = w.reshape(nb // per, per, bd, bd)
    eye = jnp.eye(per, dtype=w.dtype)
    return jnp.einsum("cghj,gk->cghkj", w4, eye).reshape(nb // per, MXU_TILE, MXU_TILE)


def kernel(x, c, w_mod, b_mod, g_pre, w_in, conv_w, conv_b, w_rg, b_rg, w_ig, b_ig,
           lru_lambda, w_pa, w_pb, w_o, g_post):
    batch, seq, d = x.shape
    depth = w_mod.shape[0]
    n = batch * seq
    tm_in, tm_post = 512, 512

    mod = _mod_call(c.T, w_mod, b_mod)
    xf = x.reshape(n, d)
    for l in range(depth):
        mod_l = mod[l].reshape(batch, 3, d)
        *a_g, rest, hb = _inproj_lru_call(
            xf, mod_l, g_pre[l][None, :], w_in[l].astype(BF16), conv_w[l], conv_b[l][None, :],
            (0.5 * _block_diag(w_rg[l])).astype(BF16), 0.5 * b_rg[l][None, :],
            (0.5 * _block_diag(w_ig[l])).astype(BF16), 0.5 * b_ig[l][None, :],
            lru_lambda[l].reshape(-1, 128), seq // tm_in, tm_in)
        os_, ms_, ls_ = [], [], []
        for g, (_, r) in enumerate(GROUPS):
            o, mx, den = _attn_call(a_g[g], g, r, batch, seq, *ATTN_TILES[g])
            os_.append(o)
            ms_.append(mx)
            ls_.append(den)
        xf = _post_call(xf, mod_l, os_, ms_, ls_, rest, hb, w_pa[l].astype(BF16),
                        w_pb[l].astype(BF16), w_o[l].astype(BF16), g_post[l][None, :],
                        seq // tm_post, tm_post)
    return xf.reshape(batch, seq, d)
```

```python
import functools

import jax
import jax.numpy as jnp
from jax import lax
from jax.experimental import pallas as pl
from jax.experimental.pallas import tpu as pltpu

F32 = jnp.float32
BF16 = jnp.bfloat16

HEAD_DIM = 128
HEADS = 4
ATT_W = HEADS * HEAD_DIM
GROUPS = ((128, 1), (512, 4), (2048, 16))
N_GROUPS = 3
BAND = 128
LRU_BLOCKS = 16
CONV_W = 4
LRU_C = 8.0
EPS = 1e-6
NEG = -1e30

QKV_W = 3 * N_GROUPS * ATT_W
CHUNK = 512

VMEM_LIMIT = 58 * 1024 * 1024
MXU_TILE = 256


def _sigmoid(v):
    return 1.0 / (1.0 + jnp.exp(-v))


def _mod_kernel(ct_ref, w_ref, b_ref, o_ref):
    ct = ct_ref[...]
    sc = ct * _sigmoid(ct)
    w = w_ref[0]
    rows = []
    for b in range(ct.shape[1]):
        rows.append(jnp.sum(sc[:, b:b + 1] * w, axis=0, keepdims=True))
    o_ref[0] = jnp.concatenate(rows, axis=0) + b_ref[0]


def _mod_call(ct, w_mod, b_mod):
    depth, d, d3 = w_mod.shape
    nb = ct.shape[1]
    tn = 1024
    return pl.pallas_call(
        _mod_kernel,
        grid=(depth, d3 // tn),
        in_specs=[
            pl.BlockSpec((d, nb), lambda l, j: (0, 0)),
            pl.BlockSpec((1, d, tn), lambda l, j: (l, 0, j)),
            pl.BlockSpec((1, 1, tn), lambda l, j: (l, 0, j)),
        ],
        out_specs=pl.BlockSpec((1, nb, tn), lambda l, j: (l, 0, j)),
        out_shape=jax.ShapeDtypeStruct((depth, nb, d3), F32),
        compiler_params=pltpu.CompilerParams(
            dimension_semantics=("parallel", "parallel"),
            vmem_limit_bytes=VMEM_LIMIT),
        name="mod",
    )(ct, w_mod, b_mod.reshape(depth, 1, d3))


J_GATT = 9
J_U = (10, 11)
J_GLRU = (12, 13)
J_MERGE = (14, 15, 16, 17)
ATTN_TILES = ((1024, 1), (1024, 1), (512, 2))
SCAN_STEPS = 16
POST_SUB = 256


def _inproj_lru_kernel(x_ref, mod_ref, gpre_ref, w_ref, cw_ref, cb_ref, wrg_ref, brg_ref,
                       wig_ref, big_ref, lam_ref,
                       a0_ref, a1_ref, a2_ref, rest_ref, hbo_ref,
                       hb_ref, hf_ref, hp1_ref, hp2_ref, ubuf, ucb, uct, rpt, ipt, glru, hcar,
                       *, tiles_per_batch):
    tm, d = x_ref.shape
    c = ucb.shape[1]

    @pl.when(pl.program_id(0) % tiles_per_batch == 0)
    def _():
        ubuf[0:8, :] = jnp.zeros((8, c), F32)
        hcar[...] = jnp.zeros(hcar.shape, F32)

    x = x_ref[...]
    ms = jnp.mean(x * x, axis=-1, keepdims=True)
    y = x * lax.rsqrt(ms + EPS) * gpre_ref[...]
    h = y * (1.0 + mod_ref[0, 1:2, :]) + mod_ref[0, 0:1, :]
    hb_ref[...] = h.astype(BF16)
    for q in range(d // 128):
        hf_ref[q] = h[:, q * 128:(q + 1) * 128]
    hp_refs = {GROUPS[1][1]: hp1_ref, GROUPS[2][1]: hp2_ref}
    for r, hp_ref in hp_refs.items():
        nl = tm // r
        for q in range(d // 128):
            for rho in range(r):
                v = hf_ref[q, pl.ds(rho, nl, stride=r), :]
                hp_ref[rho * nl:(rho + 1) * nl, q * 128:(q + 1) * 128] = v.astype(BF16)

    def proj(j, lhs_ref=hb_ref):
        return jnp.dot(lhs_ref[...], w_ref[:, j * CHUNK:(j + 1) * CHUNK],
                       preferred_element_type=F32)

    a_refs = (a0_ref, a1_ref, a2_ref)

    def tail(acc):
        return acc[tm - 8:tm, CHUNK - 128:CHUNK]

    def qkv_task(j):
        kind, g = divmod(j, N_GROUPS)
        r = GROUPS[g][1]
        if r == 1:
            acc = proj(j)
            a_refs[g][:, kind * ATT_W:(kind + 1) * ATT_W] = acc.astype(BF16)
        else:
            acc = proj(j, hp_refs[r])
            nl = tm // r
            for rho in range(r):
                c0 = (kind * r + rho) * ATT_W
                a_refs[g][:, c0:c0 + ATT_W] = acc[rho * nl:(rho + 1) * nl, :].astype(BF16)
        return tail(acc)

    def merge_task(t, j):
        acc = proj(j)
        rest_ref[:, t * CHUNK:(t + 1) * CHUNK] = _sigmoid(acc).astype(BF16)
        return tail(acc)

    def gatt_task():
        acc = proj(J_GATT)
        nm = len(J_MERGE)
        rest_ref[:, nm * CHUNK:(nm + 1) * CHUNK] = (acc * _sigmoid(acc)).astype(BF16)
        return tail(acc)

    simple = [functools.partial(merge_task, t, j) for t, j in enumerate(J_MERGE)] + [gatt_task]
    tasks = []
    for kind in range(3):
        tasks.append(functools.partial(qkv_task, kind * N_GROUPS))
    tasks += simple
    for g in (1, 2):
        for kind in range(3):
            tasks.append(functools.partial(qkv_task, kind * N_GROUPS + g))


    for t, j in enumerate(J_U):
        ubuf[8:8 + tm, t * CHUNK:(t + 1) * CHUNK] = proj(j)
    for _ in range(2):
        tasks.pop(0)()

    lb = c // 128
    cw = cw_ref[...]
    cb = cb_ref[...]
    rc = 64
    for k in range(tm // rc):
        base = 8 + k * rc
        acc = cb + cw[0:1, :] * ubuf[base:base + rc, :]
        for t in range(1, CONV_W):
            acc = acc + cw[t:t + 1, :] * ubuf[base - t:base - t + rc, :]
        ucb[k * rc:(k + 1) * rc, :] = acc.astype(BF16)
        for q in range(lb):
            uct[pl.ds(k * rc * lb + q, rc, stride=lb), :] = acc[:, q * 128:(q + 1) * 128]
    ubuf[0:8, :] = ubuf[tm:tm + 8, :]

    per = MXU_TILE // 128
    for k in range(c // MXU_TILE):
        cs = slice(k * MXU_TILE, (k + 1) * MXU_TILE)
        lhs = ucb[:, cs]
        rp = jnp.dot(lhs, wrg_ref[k], preferred_element_type=F32) + brg_ref[:, cs]
        ip = jnp.dot(lhs, wig_ref[k], preferred_element_type=F32) + big_ref[:, cs]
        for q in range(per):
            rows = pl.ds(k * per + q, tm, stride=lb)
            rpt[rows, :] = rp[:, q * 128:(q + 1) * 128]
            ipt[rows, :] = ip[:, q * 128:(q + 1) * 128]

    for t, j in enumerate(J_GLRU):
        acc = proj(j)
        glru[:, t * CHUNK:(t + 1) * CHUNK] = acc * _sigmoid(acc)

    z = -lam_ref[...]
    softplus = jnp.maximum(z, 0.0) + jnp.log1p(jnp.exp(-jnp.abs(z)))
    hn = jnp.concatenate([(-0.5 * LRU_C) * softplus] * SCAN_STEPS, axis=0)

    n_scan = tm // SCAN_STEPS
    n_tasks = len(tasks)
    issued = 0
    h = hcar[...]
    for g in range(n_scan):
        if issued < ((g + 1) * n_tasks) // n_scan:
            h = h + jnp.minimum(jnp.abs(tasks.pop(0)()), 0.0)
            issued += 1
        rows = slice(g * SCAN_STEPS * lb, (g + 1) * SCAN_STEPS * lb)
        log_a = jnp.tanh(rpt[rows, :]) * hn + hn
        a = jnp.exp(log_a)
        om = (-1.0 - a * a) * jnp.tanh(log_a)
        sq = jnp.where(om > 0.0, om * lax.rsqrt(om), 0.0)
        ig = 0.5 * jnp.tanh(ipt[rows, :]) + 0.5
        bb = (sq * uct[rows, :]) * ig
        hs = []
        for t in range(SCAN_STEPS):
            h = a[t * lb:(t + 1) * lb, :] * h + bb[t * lb:(t + 1) * lb, :]
            hs.append(h)
        rpt[rows, :] = jnp.concatenate(hs, axis=0)
        rs = slice(g * SCAN_STEPS, (g + 1) * SCAN_STEPS)
        for q in range(lb):
            cs = slice(q * 128, (q + 1) * 128)
            hv = rpt[pl.ds(g * SCAN_STEPS * lb + q, SCAN_STEPS, stride=lb), :]
            hbo_ref[rs, cs] = (hv * glru[rs, cs]).astype(BF16)
    hcar[...] = h
    for task in tasks:
        task()


def _inproj_lru_call(xf, mod, g_pre, w_in_b, conv_w, conv_b, wrg_bd, b_rg, wig_bd, b_ig,
                     lam, layer, tiles_per_batch, tm):
    n, d = xf.shape
    wn = w_in_b.shape[2]
    c = conv_w.shape[2]
    rest_w = (len(J_MERGE) + 1) * CHUNK
    nk = c // MXU_TILE
    row = lambda i: (i, 0)
    lay3 = lambda i: (layer, 0, 0)
    lay4 = lambda i: (layer, 0, 0, 0)
    return pl.pallas_call(
        functools.partial(_inproj_lru_kernel, tiles_per_batch=tiles_per_batch),
        grid=(n // tm,),
        in_specs=[
            pl.BlockSpec((tm, d), row),
            pl.BlockSpec((None, 1, 3, d), lambda i: (layer, i // tiles_per_batch, 0, 0)),
            pl.BlockSpec((None, 1, d), lay3),
            pl.BlockSpec((None, d, wn), lay3, pipeline_mode=pl.Buffered(1)),
            pl.BlockSpec((None, CONV_W, c), lay3),
            pl.BlockSpec((None, 1, c), lay3),
            pl.BlockSpec((None, nk, MXU_TILE, MXU_TILE), lay4),
            pl.BlockSpec((None, 1, c), lay3),
            pl.BlockSpec((None, nk, MXU_TILE, MXU_TILE), lay4),
            pl.BlockSpec((None, 1, c), lay3),
            pl.BlockSpec((None, c // 128, 128), lay3),
        ],
        out_specs=[
            pl.BlockSpec((tm // r, r * 3 * ATT_W), row) for _, r in GROUPS
        ] + [pl.BlockSpec((tm, rest_w), row), pl.BlockSpec((tm, c), row)],
        out_shape=[
            jax.ShapeDtypeStruct((n // r, r * 3 * ATT_W), BF16) for _, r in GROUPS
        ] + [jax.ShapeDtypeStruct((n, rest_w), BF16), jax.ShapeDtypeStruct((n, c), BF16)],
        scratch_shapes=[
            pltpu.VMEM((tm, d), BF16),
            pltpu.VMEM((d // 128, tm, 128), F32),
            pltpu.VMEM((tm, d), BF16),
            pltpu.VMEM((tm, d), BF16),
            pltpu.VMEM((tm + 8, c), F32),
            pltpu.VMEM((tm, c), BF16),
            pltpu.VMEM((tm * (c // 128), 128), F32),
            pltpu.VMEM((tm * (c // 128), 128), F32),
            pltpu.VMEM((tm * (c // 128), 128), F32),
            pltpu.VMEM((tm, c), F32),
            pltpu.VMEM((c // 128, 128), F32),
        ],
        compiler_params=pltpu.CompilerParams(
            dimension_semantics=("arbitrary",),
            vmem_limit_bytes=VMEM_LIMIT),
        name="inproj_lru",
    )(xf, mod, g_pre, w_in_b, conv_w, conv_b, wrg_bd, b_rg, wig_bd, b_ig, lam)


def _attn_kernel(q_ref, kp_ref, kc_ref, vp_ref, vc_ref, o_ref, m_ref, l_ref, *, tq, nres):
    n = pl.program_id(2)
    scale = HEAD_DIM ** -0.5
    c2 = scale * 1.4426950408889634
    row = lax.broadcasted_iota(jnp.int32, (BAND, 2 * BAND), 0)
    col = lax.broadcasted_iota(jnp.int32, (BAND, 2 * BAND), 1)
    band = (col >= row) & (col <= row + BAND)
    first = band & ((col >= BAND) | (n > 0))
    lane = lax.broadcasted_iota(jnp.int32, (BAND, 128), 1)

    def by_head(vals):
        return jnp.where(lane < 32, vals[0],
                         jnp.where(lane < 64, vals[1],
                                   jnp.where(lane < 96, vals[2], vals[3])))

    for rr in range(nres):
        for i in range(tq // BAND):
            rs = slice(i * BAND, (i + 1) * BAND)
            ms, ls = [], []
            for h in range(HEADS):
                hs = slice(rr * ATT_W + h * HEAD_DIM, rr * ATT_W + (h + 1) * HEAD_DIM)
                q = q_ref[rs, hs]
                if i == 0:
                    kk = jnp.concatenate([kp_ref[:, hs], kc_ref[0:BAND, hs]], axis=0)
                    vv = jnp.concatenate([vp_ref[:, hs], vc_ref[0:BAND, hs]], axis=0)
                    mask = first
                else:
                    kk = kc_ref[(i - 1) * BAND:(i + 1) * BAND, hs]
                    vv = vc_ref[(i - 1) * BAND:(i + 1) * BAND, hs]
                    mask = band
                s = lax.dot_general(q, kk, (((1,), (1,)), ((), ())),
                                    preferred_element_type=F32)
                s = jnp.where(mask, s, NEG)
                m = jnp.max(s, axis=-1, keepdims=True)
                p = jnp.exp2((s - m) * c2)
                l = jnp.sum(p, axis=-1, keepdims=True)
                o = jnp.dot(p.astype(BF16), vv, preferred_element_type=F32)
                o_ref[rs, hs] = o.astype(BF16)
                ms.append(jnp.broadcast_to(m * scale, (BAND, 128)))
                ls.append(jnp.broadcast_to(l, (BAND, 128)))
            m_ref[rs, rr * 128:(rr + 1) * 128] = by_head(ms)
            l_ref[rs, rr * 128:(rr + 1) * 128] = by_head(ls)


def _attn_call(a, g, r, batch, seq, tq, nres):
    sub = seq // r
    nblk = sub // tq
    per = tq // BAND
    nrb = r // nres

    def cur(kind):
        return lambda b, rb, i: (b * nblk + i, kind * nrb + rb)

    def prev(kind):
        return lambda b, rb, i: (b * nblk * per + jnp.maximum(i * per - 1, 0),
                                 kind * nrb + rb)

    out_map = lambda b, rb, i: (b * nblk + i, rb)
    return pl.pallas_call(
        functools.partial(_attn_kernel, tq=tq, nres=nres),
        grid=(batch, nrb, nblk),
        in_specs=[
            pl.BlockSpec((tq, nres * ATT_W), cur(0)),
            pl.BlockSpec((BAND, nres * ATT_W), prev(1)),
            pl.BlockSpec((tq, nres * ATT_W), cur(1)),
            pl.BlockSpec((BAND, nres * ATT_W), prev(2)),
            pl.BlockSpec((tq, nres * ATT_W), cur(2)),
        ],
        out_specs=[
            pl.BlockSpec((tq, nres * ATT_W), out_map),
            pl.BlockSpec((tq, nres * 128), out_map),
            pl.BlockSpec((tq, nres * 128), out_map),
        ],
        out_shape=[
            jax.ShapeDtypeStruct((batch * sub, r * ATT_W), BF16),
            jax.ShapeDtypeStruct((batch * sub, r * 128), F32),
            jax.ShapeDtypeStruct((batch * sub, r * 128), F32),
        ],
        compiler_params=pltpu.CompilerParams(
            dimension_semantics=("parallel", "parallel", "parallel"),
            vmem_limit_bytes=VMEM_LIMIT),
        name=f"attn_g{g}",
    )(a, a, a, a, a)


def _post_kernel(x_ref, mod_ref, o0_ref, o1_ref, o2_ref, m0_ref, m1_ref, m2_ref,
                 l0_ref, l1_ref, l2_ref, gatt_ref, hb_ref, mrg_ref, wpa_ref, wpb_ref, wo_ref,
                 gpost_ref, out_ref, onat_ref, snat_ref):
    tm = x_ref.shape[0]
    d = x_ref.shape[1]
    for s0 in range(0, tm, POST_SUB):
        rs = slice(s0, s0 + POST_SUB)
        for g, (o_ref, m_ref, l_ref) in enumerate(((o1_ref, m1_ref, l1_ref),
                                                   (o2_ref, m2_ref, l2_ref))):
            r = GROUPS[g + 1][1]
            ls = slice(s0 // r, (s0 + POST_SUB) // r)
            for rho in range(r):
                rows = pl.ds(s0 + rho, POST_SUB // r, stride=r)
                snat_ref[g, 0, rows, :] = m_ref[ls, rho * 128:(rho + 1) * 128]
                snat_ref[g, 1, rows, :] = l_ref[ls, rho * 128:(rho + 1) * 128]
                for h in range(HEADS):
                    c0 = rho * ATT_W + h * HEAD_DIM
                    onat_ref[g, h, rows, :] = o_ref[ls, c0:c0 + HEAD_DIM].astype(F32)
        ma, mb, mc = m0_ref[rs, :], snat_ref[0, 0, rs, :], snat_ref[1, 0, rs, :]
        mx = jnp.maximum(jnp.maximum(ma, mb), mc)
        ea, eb, ec = jnp.exp(ma - mx), jnp.exp(mb - mx), jnp.exp(mc - mx)
        inv = 1.0 / (ea * l0_ref[rs, :] + eb * snat_ref[0, 1, rs, :] + ec * snat_ref[1, 1, rs, :])
        ws = (ea * inv, eb * inv, ec * inv)
        pieces = []
        for h in range(HEADS):
            hs = slice(h * HEAD_DIM, (h + 1) * HEAD_DIM)
            ogs = (o0_ref[rs, hs].astype(F32), onat_ref[0, h, rs, :], onat_ref[1, h, rs, :])
            acc = None
            for w, og in zip(ws, ogs):
                wb = jnp.broadcast_to(w[:, 32 * h:32 * h + 1], (POST_SUB, HEAD_DIM))
                t = wb * og
                acc = t if acc is None else acc + t
            pieces.append((acc * gatt_ref[rs, hs].astype(F32)).astype(BF16))
        ab = jnp.concatenate(pieces, axis=1)
        ya = jnp.dot(ab, wpa_ref[...], preferred_element_type=F32)
        yb = jnp.dot(hb_ref[rs, :], wpb_ref[...], preferred_element_type=F32)
        zz = mrg_ref[rs, 0:d].astype(F32) * ya + mrg_ref[rs, d:2 * d].astype(F32) * yb
        out = jnp.dot(zz.astype(BF16), wo_ref[...], preferred_element_type=F32)
        ms = jnp.mean(out * out, axis=-1, keepdims=True)
        r = out * lax.rsqrt(ms + EPS) * gpost_ref[...]
        out_ref[rs, :] = x_ref[rs, :] + mod_ref[0, 2:3, :] * r


def _post_call(xf, mod, os_, ms_, ls_, rest, hb, wpa_b, wpb_b, wo_b, g_post,
               layer, tiles_per_batch, tm):
    n, d = xf.shape
    row = lambda i: (i, 0)
    gatt_blk = lambda i: (i, (2 * d) // ATT_W)
    lay3 = lambda i: (layer, 0, 0)
    return pl.pallas_call(
        _post_kernel,
        grid=(n // tm,),
        in_specs=[
            pl.BlockSpec((tm, d), row),
            pl.BlockSpec((None, 1, 3, d), lambda i: (layer, i // tiles_per_batch, 0, 0)),
        ] + [pl.BlockSpec((tm // r, r * ATT_W), row) for _, r in GROUPS
        ] + [pl.BlockSpec((tm // r, r * 128), row) for _, r in GROUPS
        ] + [pl.BlockSpec((tm // r, r * 128), row) for _, r in GROUPS
        ] + [
            pl.BlockSpec((tm, ATT_W), gatt_blk),
            pl.BlockSpec((tm, d), row),
            pl.BlockSpec((tm, 2 * d), row),
            pl.BlockSpec((None, ATT_W, d), lay3),
            pl.BlockSpec((None, d, d), lay3),
            pl.BlockSpec((None, d, d), lay3),
            pl.BlockSpec((None, 1, d), lay3),
        ],
        out_specs=pl.BlockSpec((tm, d), row),
        out_shape=jax.ShapeDtypeStruct((n, d), F32),
        scratch_shapes=[pltpu.VMEM((N_GROUPS - 1, HEADS, tm, HEAD_DIM), F32),
                        pltpu.VMEM((N_GROUPS - 1, 2, tm, 128), F32)],
        compiler_params=pltpu.CompilerParams(
            dimension_semantics=("parallel",),
            vmem_limit_bytes=VMEM_LIMIT),
        name="post",
    )(xf, mod, *os_, *ms_, *ls_, rest, hb, rest, wpa_b, wpb_b, wo_b, g_post)


def _block_diag(w):
    depth, nb, bd, _ = w.shape
    per = MXU_TILE // bd
    w4 = w.reshape(depth, nb // per, per, bd, bd)
    eye = jnp.eye(per, dtype=w.dtype)
    return jnp.einsum("lcghj,gk->lcghkj", w4, eye).reshape(depth, nb // per, MXU_TILE, MXU_TILE)


def kernel(x, c, w_mod, b_mod, g_pre, w_in, conv_w, conv_b, w_rg, b_rg, w_ig, b_ig,
           lru_lambda, w_pa, w_pb, w_o, g_post):
    batch, seq, d = x.shape
    depth = w_mod.shape[0]
    n = batch * seq
    tm_in, tm_post = 512, 512

    mod = _mod_call(c.T, w_mod, b_mod).reshape(depth, batch, 3, d)
    row3 = lambda p: p.reshape(depth, 1, -1)
    w_in_b, wpa_b, wpb_b, wo_b = (w.astype(BF16) for w in (w_in, w_pa, w_pb, w_o))
    wrg_bd = (0.5 * _block_diag(w_rg)).astype(BF16)
    wig_bd = (0.5 * _block_diag(w_ig)).astype(BF16)
    lam = lru_lambda.reshape(depth, -1, 128)
    xf = x.reshape(n, d)
    for l in range(depth):
        *a_g, rest, hb = _inproj_lru_call(
            xf, mod, row3(g_pre), w_in_b, conv_w, row3(conv_b), wrg_bd, 0.5 * row3(b_rg),
            wig_bd, 0.5 * row3(b_ig), lam, l, seq // tm_in, tm_in)
        os_, ms_, ls_ = [], [], []
        for g, (_, r) in enumerate(GROUPS):
            o, mx, den = _attn_call(a_g[g], g, r, batch, seq, *ATTN_TILES[g])
            os_.append(o)
            ms_.append(mx)
            ls_.append(den)
        xf = _post_call(xf, mod, os_, ms_, ls_, rest, hb, wpa_b, wpb_b, wo_b, row3(g_post),
                        l, seq // tm_post, tm_post)
    return xf.reshape(batch, seq, d)
```

```python
import functools

import jax
import jax.numpy as jnp
from jax import lax
from jax.experimental import pallas as pl
from jax.experimental.pallas import tpu as pltpu

F32 = jnp.float32
BF16 = jnp.bfloat16

HEAD_DIM = 128
HEADS = 4
ATT_W = HEADS * HEAD_DIM
GROUPS = ((128, 1), (512, 4), (2048, 16))
N_GROUPS = 3
BAND = 128
LRU_BLOCKS = 16
CONV_W = 4
LRU_C = 8.0
EPS = 1e-6
NEG = -1e30

QKV_W = 3 * N_GROUPS * ATT_W
CHUNK = 512

VMEM_LIMIT = 58 * 1024 * 1024
MXU_TILE = 256


def _sigmoid(v):
    return 1.0 / (1.0 + jnp.exp(-v))


def _mod_kernel(ct_ref, w_ref, b_ref, o_ref):
    ct = ct_ref[...]
    sc = ct * _sigmoid(ct)
    w = w_ref[0]
    rows = []
    for b in range(ct.shape[1]):
        rows.append(jnp.sum(sc[:, b:b + 1] * w, axis=0, keepdims=True))
    o_ref[0] = jnp.concatenate(rows, axis=0) + b_ref[0]


def _mod_call(ct, w_mod, b_mod):
    depth, d, d3 = w_mod.shape
    nb = ct.shape[1]
    tn = 1024
    return pl.pallas_call(
        _mod_kernel,
        grid=(depth, d3 // tn),
        in_specs=[
            pl.BlockSpec((d, nb), lambda l, j: (0, 0)),
            pl.BlockSpec((1, d, tn), lambda l, j: (l, 0, j)),
            pl.BlockSpec((1, 1, tn), lambda l, j: (l, 0, j)),
        ],
        out_specs=pl.BlockSpec((1, nb, tn), lambda l, j: (l, 0, j)),
        out_shape=jax.ShapeDtypeStruct((depth, nb, d3), F32),
        compiler_params=pltpu.CompilerParams(
            dimension_semantics=("parallel", "parallel"),
            vmem_limit_bytes=VMEM_LIMIT),
        name="mod",
    )(ct, w_mod, b_mod.reshape(depth, 1, d3))


J_GATT = 9
J_U = (10, 11)
J_GLRU = (12, 13)
J_MERGE = (14, 15, 16, 17)
ATTN_TILES = ((1024, 1), (1024, 1), (512, 2))
SCAN_STEPS = 16
POST_SUB = 256


def _inproj_lru_kernel(x_ref, mod_ref, gpre_ref, w_ref, cw_ref, cb_ref, wrg_ref, brg_ref,
                       wig_ref, big_ref, lam_ref,
                       a0_ref, a1_ref, a2_ref, rest_ref, hbo_ref,
                       hb_ref, hf_ref, hp1_ref, hp2_ref, ubuf, ucb, uct, rpt, ipt, glru, hcar,
                       *, tiles_per_batch):
    tm, d = x_ref.shape
    c = ucb.shape[1]

    @pl.when(pl.program_id(0) % tiles_per_batch == 0)
    def _():
        ubuf[0:8, :] = jnp.zeros((8, c), F32)
        hcar[...] = jnp.zeros(hcar.shape, F32)

    x = x_ref[...]
    ms = jnp.mean(x * x, axis=-1, keepdims=True)
    y = x * lax.rsqrt(ms + EPS) * gpre_ref[...]
    h = y * (1.0 + mod_ref[0, 1:2, :]) + mod_ref[0, 0:1, :]
    hb_ref[...] = h.astype(BF16)
    for q in range(d // 128):
        hf_ref[q] = h[:, q * 128:(q + 1) * 128]
    hp_refs = {GROUPS[1][1]: hp1_ref, GROUPS[2][1]: hp2_ref}
    for r, hp_ref in hp_refs.items():
        nl = tm // r
        for q in range(d // 128):
            for rho in range(r):
                v = hf_ref[q, pl.ds(rho, nl, stride=r), :]
                hp_ref[rho * nl:(rho + 1) * nl, q * 128:(q + 1) * 128] = v.astype(BF16)

    def proj(j, lhs_ref=hb_ref):
        return jnp.dot(lhs_ref[...], w_ref[:, j * CHUNK:(j + 1) * CHUNK],
                       preferred_element_type=F32)

    a_refs = (a0_ref, a1_ref, a2_ref)

    def tail(acc):
        return acc[tm - 8:tm, CHUNK - 128:CHUNK]

    def qkv_task(j):
        kind, g = divmod(j, N_GROUPS)
        r = GROUPS[g][1]
        if r == 1:
            acc = proj(j)
            a_refs[g][:, kind * ATT_W:(kind + 1) * ATT_W] = acc.astype(BF16)
        else:
            acc = proj(j, hp_refs[r])
            nl = tm // r
            for rho in range(r):
                c0 = (kind * r + rho) * ATT_W
                a_refs[g][:, c0:c0 + ATT_W] = acc[rho * nl:(rho + 1) * nl, :].astype(BF16)
        return tail(acc)

    def merge_task(t, j):
        acc = proj(j)
        rest_ref[:, t * CHUNK:(t + 1) * CHUNK] = _sigmoid(acc).astype(BF16)
        return tail(acc)

    def gatt_task():
        acc = proj(J_GATT)
        nm = len(J_MERGE)
        rest_ref[:, nm * CHUNK:(nm + 1) * CHUNK] = (acc * _sigmoid(acc)).astype(BF16)
        return tail(acc)

    simple = [functools.partial(merge_task, t, j) for t, j in enumerate(J_MERGE)] + [gatt_task]
    tasks = []
    for kind in range(3):
        tasks.append(functools.partial(qkv_task, kind * N_GROUPS))
    tasks += simple
    for g in (1, 2):
        for kind in range(3):
            tasks.append(functools.partial(qkv_task, kind * N_GROUPS + g))


    for t, j in enumerate(J_U):
        ubuf[8:8 + tm, t * CHUNK:(t + 1) * CHUNK] = proj(j)
    for _ in range(2):
        tasks.pop(0)()

    lb = c // 128
    cw = cw_ref[...]
    cb = cb_ref[...]
    rc = 64
    for k in range(tm // rc):
        base = 8 + k * rc
        acc = cb + cw[0:1, :] * ubuf[base:base + rc, :]
        for t in range(1, CONV_W):
            acc = acc + cw[t:t + 1, :] * ubuf[base - t:base - t + rc, :]
        ucb[k * rc:(k + 1) * rc, :] = acc.astype(BF16)
        for q in range(lb):
            uct[pl.ds(k * rc * lb + q, rc, stride=lb), :] = acc[:, q * 128:(q + 1) * 128]
    ubuf[0:8, :] = ubuf[tm:tm + 8, :]

    per = MXU_TILE // 128
    for k in range(c // MXU_TILE):
        cs = slice(k * MXU_TILE, (k + 1) * MXU_TILE)
        lhs = ucb[:, cs]
        rp = jnp.dot(lhs, wrg_ref[k], preferred_element_type=F32) + brg_ref[:, cs]
        ip = jnp.dot(lhs, wig_ref[k], preferred_element_type=F32) + big_ref[:, cs]
        for q in range(per):
            rows = pl.ds(k * per + q, tm, stride=lb)
            rpt[rows, :] = rp[:, q * 128:(q + 1) * 128]
            ipt[rows, :] = ip[:, q * 128:(q + 1) * 128]

    for t, j in enumerate(J_GLRU):
        acc = proj(j)
        glru[:, t * CHUNK:(t + 1) * CHUNK] = acc * _sigmoid(acc)

    z = -lam_ref[...]
    softplus = jnp.maximum(z, 0.0) + jnp.log1p(jnp.exp(-jnp.abs(z)))
    hn = jnp.concatenate([(-0.5 * LRU_C) * softplus] * SCAN_STEPS, axis=0)

    n_scan = tm // SCAN_STEPS
    n_tasks = len(tasks)
    issued = 0
    h = hcar[...]
    for g in range(n_scan):
        if issued < ((g + 1) * n_tasks) // n_scan:
            h = h + jnp.minimum(jnp.abs(tasks.pop(0)()), 0.0)
            issued += 1
        rows = slice(g * SCAN_STEPS * lb, (g + 1) * SCAN_STEPS * lb)
        log_a = jnp.tanh(rpt[rows, :]) * hn + hn
        a = jnp.exp(log_a)
        om = (-1.0 - a * a) * jnp.tanh(log_a)
        sq = jnp.where(om > 0.0, om * lax.rsqrt(om), 0.0)
        ig = 0.5 * jnp.tanh(ipt[rows, :]) + 0.5
        bb = (sq * uct[rows, :]) * ig
        hs = []
        for t in range(SCAN_STEPS):
            h = a[t * lb:(t + 1) * lb, :] * h + bb[t * lb:(t + 1) * lb, :]
            hs.append(h)
        rpt[rows, :] = jnp.concatenate(hs, axis=0)
        rs = slice(g * SCAN_STEPS, (g + 1) * SCAN_STEPS)
        for q in range(lb):
            cs = slice(q * 128, (q + 1) * 128)
            hv = rpt[pl.ds(g * SCAN_STEPS * lb + q, SCAN_STEPS, stride=lb), :]
            hbo_ref[rs, cs] = (hv * glru[rs, cs]).astype(BF16)
    hcar[...] = h
    for task in tasks:
        task()


def _inproj_lru_call(xf, mod, g_pre, w_in_b, conv_w, conv_b, wrg_bd, b_rg, wig_bd, b_ig,
                     lam, layer, tiles_per_batch, tm):
    n, d = xf.shape
    wn = w_in_b.shape[2]
    c = conv_w.shape[2]
    rest_w = (len(J_MERGE) + 1) * CHUNK
    nk = c // MXU_TILE
    row = lambda i: (i, 0)
    lay3 = lambda i: (layer, 0, 0)
    lay4 = lambda i: (layer, 0, 0, 0)
    return pl.pallas_call(
        functools.partial(_inproj_lru_kernel, tiles_per_batch=tiles_per_batch),
        grid=(n // tm,),
        in_specs=[
            pl.BlockSpec((tm, d), row),
            pl.BlockSpec((None, 1, 3, d), lambda i: (layer, i // tiles_per_batch, 0, 0)),
            pl.BlockSpec((None, 1, d), lay3),
            pl.BlockSpec((None, d, wn), lay3, pipeline_mode=pl.Buffered(1)),
            pl.BlockSpec((None, CONV_W, c), lay3),
            pl.BlockSpec((None, 1, c), lay3),
            pl.BlockSpec((None, nk, MXU_TILE, MXU_TILE), lay4),
            pl.BlockSpec((None, 1, c), lay3),
            pl.BlockSpec((None, nk, MXU_TILE, MXU_TILE), lay4),
            pl.BlockSpec((None, 1, c), lay3),
            pl.BlockSpec((None, c // 128, 128), lay3),
        ],
        out_specs=[
            pl.BlockSpec((tm // r, r * 3 * ATT_W), row) for _, r in GROUPS
        ] + [pl.BlockSpec((tm, rest_w), row), pl.BlockSpec((tm, c), row)],
        out_shape=[
            jax.ShapeDtypeStruct((n // r, r * 3 * ATT_W), BF16) for _, r in GROUPS
        ] + [jax.ShapeDtypeStruct((n, rest_w), BF16), jax.ShapeDtypeStruct((n, c), BF16)],
        scratch_shapes=[
            pltpu.VMEM((tm, d), BF16),
            pltpu.VMEM((d // 128, tm, 128), F32),
            pltpu.VMEM((tm, d), BF16),
            pltpu.VMEM((tm, d), BF16),
            pltpu.VMEM((tm + 8, c), F32),
            pltpu.VMEM((tm, c), BF16),
            pltpu.VMEM((tm * (c // 128), 128), F32),
            pltpu.VMEM((tm * (c // 128), 128), F32),
            pltpu.VMEM((tm * (c // 128), 128), F32),
            pltpu.VMEM((tm, c), F32),
            pltpu.VMEM((c // 128, 128), F32),
        ],
        compiler_params=pltpu.CompilerParams(
            dimension_semantics=("arbitrary",),
            vmem_limit_bytes=VMEM_LIMIT),
        name="inproj_lru",
    )(xf, mod, g_pre, w_in_b, conv_w, conv_b, wrg_bd, b_rg, wig_bd, b_ig, lam)


def _attn_kernel(q_ref, kp_ref, kc_ref, vp_ref, vc_ref, o_ref, m_ref, l_ref, bias_ref,
                 *, tq, nres):
    row = lax.broadcasted_iota(jnp.int32, (BAND, 2 * BAND), 0)
    col = lax.broadcasted_iota(jnp.int32, (BAND, 2 * BAND), 1)
    band = (col >= row) & (col <= row + BAND)
    first = band & ((col >= BAND) | (pl.program_id(2) > 0))
    bias_ref[0] = jnp.where(band, 0.0, NEG)
    bias_ref[1] = jnp.where(first, 0.0, NEG)
    ones = jnp.ones((2 * BAND, HEAD_DIM), BF16)
    ln2 = 0.6931471805599453

    for rr in range(nres):
        for i in range(tq // BAND):
            rs = slice(i * BAND, (i + 1) * BAND)
            for h in range(HEADS):
                hs = slice(rr * ATT_W + h * HEAD_DIM, rr * ATT_W + (h + 1) * HEAD_DIM)
                q = q_ref[rs, hs]
                if i == 0:
                    kk = jnp.concatenate([kp_ref[:, hs], kc_ref[0:BAND, hs]], axis=0)
                    vv = jnp.concatenate([vp_ref[:, hs], vc_ref[0:BAND, hs]], axis=0)
                else:
                    kk = kc_ref[(i - 1) * BAND:(i + 1) * BAND, hs]
                    vv = vc_ref[(i - 1) * BAND:(i + 1) * BAND, hs]
                s = lax.dot_general(q, kk, (((1,), (1,)), ((), ())),
                                    preferred_element_type=F32)
                s = s + bias_ref[1 if i == 0 else 0]
                m = jnp.max(s, axis=-1, keepdims=True)
                p = jnp.exp2(s - m).astype(BF16)
                oe = jnp.dot(p, jnp.concatenate([vv, ones], axis=1),
                             preferred_element_type=F32)
                o_ref[rs, hs] = oe[:, 0:HEAD_DIM].astype(BF16)
                ls = slice(32 * h, 32 * (h + 1))
                cs = slice(rr * 128 + 32 * h, rr * 128 + 32 * (h + 1))
                m_ref[rs, cs] = jnp.broadcast_to(m * ln2, (BAND, 128))[:, ls]
                l_ref[rs, cs] = oe[:, HEAD_DIM:2 * HEAD_DIM][:, ls]


def _attn_call(a, g, r, batch, seq, tq, nres):
    sub = seq // r
    nblk = sub // tq
    per = tq // BAND
    nrb = r // nres

    def cur(kind):
        return lambda b, rb, i: (b * nblk + i, kind * nrb + rb)

    def prev(kind):
        return lambda b, rb, i: (b * nblk * per + jnp.maximum(i * per - 1, 0),
                                 kind * nrb + rb)

    out_map = lambda b, rb, i: (b * nblk + i, rb)
    return pl.pallas_call(
        functools.partial(_attn_kernel, tq=tq, nres=nres),
        grid=(batch, nrb, nblk),
        in_specs=[
            pl.BlockSpec((tq, nres * ATT_W), cur(0)),
            pl.BlockSpec((BAND, nres * ATT_W), prev(1)),
            pl.BlockSpec((tq, nres * ATT_W), cur(1)),
            pl.BlockSpec((BAND, nres * ATT_W), prev(2)),
            pl.BlockSpec((tq, nres * ATT_W), cur(2)),
        ],
        out_specs=[
            pl.BlockSpec((tq, nres * ATT_W), out_map),
            pl.BlockSpec((tq, nres * 128), out_map),
            pl.BlockSpec((tq, nres * 128), out_map),
        ],
        out_shape=[
            jax.ShapeDtypeStruct((batch * sub, r * ATT_W), BF16),
            jax.ShapeDtypeStruct((batch * sub, r * 128), F32),
            jax.ShapeDtypeStruct((batch * sub, r * 128), F32),
        ],
        scratch_shapes=[pltpu.VMEM((2, BAND, 2 * BAND), F32)],
        compiler_params=pltpu.CompilerParams(
            dimension_semantics=("parallel", "parallel", "parallel"),
            vmem_limit_bytes=VMEM_LIMIT),
        name=f"attn_g{g}",
    )(a, a, a, a, a)


def _post_kernel(x_ref, mod_ref, o0_ref, o1_ref, o2_ref, m0_ref, m1_ref, m2_ref,
                 l0_ref, l1_ref, l2_ref, gatt_ref, hb_ref, mrg_ref, wpa_ref, wpb_ref, wo_ref,
                 gpost_ref, out_ref, onat_ref, snat_ref):
    tm = x_ref.shape[0]
    d = x_ref.shape[1]
    for s0 in range(0, tm, POST_SUB):
        rs = slice(s0, s0 + POST_SUB)
        for g, (o_ref, m_ref, l_ref) in enumerate(((o1_ref, m1_ref, l1_ref),
                                                   (o2_ref, m2_ref, l2_ref))):
            r = GROUPS[g + 1][1]
            ls = slice(s0 // r, (s0 + POST_SUB) // r)
            for rho in range(r):
                rows = pl.ds(s0 + rho, POST_SUB // r, stride=r)
                snat_ref[g, 0, rows, :] = m_ref[ls, rho * 128:(rho + 1) * 128]
                snat_ref[g, 1, rows, :] = l_ref[ls, rho * 128:(rho + 1) * 128]
                for h in range(HEADS):
                    c0 = rho * ATT_W + h * HEAD_DIM
                    onat_ref[g, h, rows, :] = o_ref[ls, c0:c0 + HEAD_DIM].astype(F32)
        ma, mb, mc = m0_ref[rs, :], snat_ref[0, 0, rs, :], snat_ref[1, 0, rs, :]
        mx = jnp.maximum(jnp.maximum(ma, mb), mc)
        ea, eb, ec = jnp.exp(ma - mx), jnp.exp(mb - mx), jnp.exp(mc - mx)
        inv = 1.0 / (ea * l0_ref[rs, :] + eb * snat_ref[0, 1, rs, :] + ec * snat_ref[1, 1, rs, :])
        ws = (ea * inv, eb * inv, ec * inv)
        pieces = []
        for h in range(HEADS):
            hs = slice(h * HEAD_DIM, (h + 1) * HEAD_DIM)
            ogs = (o0_ref[rs, hs].astype(F32), onat_ref[0, h, rs, :], onat_ref[1, h, rs, :])
            acc = None
            for w, og in zip(ws, ogs):
                wb = jnp.broadcast_to(w[:, 32 * h:32 * h + 1], (POST_SUB, HEAD_DIM))
                t = wb * og
                acc = t if acc is None else acc + t
            pieces.append((acc * gatt_ref[rs, hs].astype(F32)).astype(BF16))
        ab = jnp.concatenate(pieces, axis=1)
        ya = jnp.dot(ab, wpa_ref[...], preferred_element_type=F32)
        yb = jnp.dot(hb_ref[rs, :], wpb_ref[...], preferred_element_type=F32)
        zz = mrg_ref[rs, 0:d].astype(F32) * ya + mrg_ref[rs, d:2 * d].astype(F32) * yb
        out = jnp.dot(zz.astype(BF16), wo_ref[...], preferred_element_type=F32)
        ms = jnp.mean(out * out, axis=-1, keepdims=True)
        r = out * lax.rsqrt(ms + EPS) * gpost_ref[...]
        out_ref[rs, :] = x_ref[rs, :] + mod_ref[0, 2:3, :] * r


def _post_call(xf, mod, os_, ms_, ls_, rest, hb, wpa_b, wpb_b, wo_b, g_post,
               layer, tiles_per_batch, tm):
    n, d = xf.shape
    row = lambda i: (i, 0)
    gatt_blk = lambda i: (i, (2 * d) // ATT_W)
    lay3 = lambda i: (layer, 0, 0)
    return pl.pallas_call(
        _post_kernel,
        grid=(n // tm,),
        in_specs=[
            pl.BlockSpec((tm, d), row),
            pl.BlockSpec((None, 1, 3, d), lambda i: (layer, i // tiles_per_batch, 0, 0)),
        ] + [pl.BlockSpec((tm // r, r * ATT_W), row) for _, r in GROUPS
        ] + [pl.BlockSpec((tm // r, r * 128), row) for _, r in GROUPS
        ] + [pl.BlockSpec((tm // r, r * 128), row) for _, r in GROUPS
        ] + [
            pl.BlockSpec((tm, ATT_W), gatt_blk),
            pl.BlockSpec((tm, d), row),
            pl.BlockSpec((tm, 2 * d), row),
            pl.BlockSpec((None, ATT_W, d), lay3),
            pl.BlockSpec((None, d, d), lay3),
            pl.BlockSpec((None, d, d), lay3),
            pl.BlockSpec((None, 1, d), lay3),
        ],
        out_specs=pl.BlockSpec((tm, d), row),
        out_shape=jax.ShapeDtypeStruct((n, d), F32),
        scratch_shapes=[pltpu.VMEM((N_GROUPS - 1, HEADS, tm, HEAD_DIM), F32),
                        pltpu.VMEM((N_GROUPS - 1, 2, tm, 128), F32)],
        compiler_params=pltpu.CompilerParams(
            dimension_semantics=("parallel",),
            vmem_limit_bytes=VMEM_LIMIT),
        name="post",
    )(xf, mod, *os_, *ms_, *ls_, rest, hb, rest, wpa_b, wpb_b, wo_b, g_post)


def _block_diag(w):
    depth, nb, bd, _ = w.shape
    per = MXU_TILE // bd
    w4 = w.reshape(depth, nb // per, per, bd, bd)
    eye = jnp.eye(per, dtype=w.dtype)
    return jnp.einsum("lcghj,gk->lcghkj", w4, eye).reshape(depth, nb // per, MXU_TILE, MXU_TILE)


def kernel(x, c, w_mod, b_mod, g_pre, w_in, conv_w, conv_b, w_rg, b_rg, w_ig, b_ig,
           lru_lambda, w_pa, w_pb, w_o, g_post):
    batch, seq, d = x.shape
    depth = w_mod.shape[0]
    n = batch * seq
    tm_in, tm_post = 512, 512

    mod = _mod_call(c.T, w_mod, b_mod).reshape(depth, batch, 3, d)
    row3 = lambda p: p.reshape(depth, 1, -1)
    qscale = jnp.where(jnp.arange(w_in.shape[2]) < N_GROUPS * ATT_W,
                       HEAD_DIM ** -0.5 * 1.4426950408889634, 1.0).astype(F32)
    w_in_b = (w_in * qscale).astype(BF16)
    wpa_b, wpb_b, wo_b = (w.astype(BF16) for w in (w_pa, w_pb, w_o))
    wrg_bd = (0.5 * _block_diag(w_rg)).astype(BF16)
    wig_bd = (0.5 * _block_diag(w_ig)).astype(BF16)
    lam = lru_lambda.reshape(depth, -1, 128)
    xf = x.reshape(n, d)
    for l in range(depth):
        *a_g, rest, hb = _inproj_lru_call(
            xf, mod, row3(g_pre), w_in_b, conv_w, row3(conv_b), wrg_bd, 0.5 * row3(b_rg),
            wig_bd, 0.5 * row3(b_ig), lam, l, seq // tm_in, tm_in)
        os_, ms_, ls_ = [], [], []
        for g, (_, r) in enumerate(GROUPS):
            o, mx, den = _attn_call(a_g[g], g, r, batch, seq, *ATTN_TILES[g])
            os_.append(o)
            ms_.append(mx)
            ls_.append(den)
        xf = _post_call(xf, mod, os_, ms_, ls_, rest, hb, wpa_b, wpb_b, wo_b, row3(g_post),
                        l, seq // tm_post, tm_post)
    return xf.reshape(batch, seq, d)
```

```python
import functools

import jax
import jax.numpy as jnp
from jax import lax
from jax.experimental import pallas as pl
from jax.experimental.pallas import tpu as pltpu

F32 = jnp.float32
BF16 = jnp.bfloat16

HEAD_DIM = 128
HEADS = 4
ATT_W = HEADS * HEAD_DIM
GROUPS = ((128, 1), (512, 4), (2048, 16))
N_GROUPS = 3
BAND = 128
LRU_BLOCKS = 16
CONV_W = 4
LRU_C = 8.0
EPS = 1e-6
NEG = -1e30

QKV_W = 3 * N_GROUPS * ATT_W
CHUNK = 512

VMEM_LIMIT = 58 * 1024 * 1024
MXU_TILE = 256


def _sigmoid(v):
    return 1.0 / (1.0 + jnp.exp(-v))


def _mod_kernel(ct_ref, w_ref, b_ref, o_ref):
    ct = ct_ref[...]
    sc = ct * _sigmoid(ct)
    w = w_ref[0]
    rows = []
    for b in range(ct.shape[1]):
        rows.append(jnp.sum(sc[:, b:b + 1] * w, axis=0, keepdims=True))
    o_ref[0] = jnp.concatenate(rows, axis=0) + b_ref[0]


def _mod_call(ct, w_mod, b_mod):
    depth, d, d3 = w_mod.shape
    nb = ct.shape[1]
    tn = 1024
    return pl.pallas_call(
        _mod_kernel,
        grid=(depth, d3 // tn),
        in_specs=[
            pl.BlockSpec((d, nb), lambda l, j: (0, 0)),
            pl.BlockSpec((1, d, tn), lambda l, j: (l, 0, j)),
            pl.BlockSpec((1, 1, tn), lambda l, j: (l, 0, j)),
        ],
        out_specs=pl.BlockSpec((1, nb, tn), lambda l, j: (l, 0, j)),
        out_shape=jax.ShapeDtypeStruct((depth, nb, d3), F32),
        compiler_params=pltpu.CompilerParams(
            dimension_semantics=("parallel", "parallel"),
            vmem_limit_bytes=VMEM_LIMIT),
        name="mod",
    )(ct, w_mod, b_mod.reshape(depth, 1, d3))


J_GATT = 9
J_U = (10, 11)
J_GLRU = (12, 13)
J_MERGE = (14, 15, 16, 17)
ATTN_TILES = ((1024, 1), (1024, 1), (512, 2))
SCAN_STEPS = 16
POST_SUB = 256


def _inproj_lru_kernel(x_ref, mod_ref, gpre_ref, w_ref, cw_ref, cb_ref, wrg_ref, brg_ref,
                       wig_ref, big_ref, lam_ref,
                       a0_ref, a1_ref, a2_ref, rest_ref, hbo_ref,
                       hb_ref, hf_ref, hp1_ref, hp2_ref, ubuf, ucb, uct, rpt, ipt, glru, hcar,
                       *, tiles_per_batch):
    tm, d = x_ref.shape
    c = ucb.shape[1]

    @pl.when(pl.program_id(0) % tiles_per_batch == 0)
    def _():
        ubuf[0:8, :] = jnp.zeros((8, c), F32)
        hcar[...] = jnp.zeros(hcar.shape, F32)

    x = x_ref[...]
    ms = jnp.mean(x * x, axis=-1, keepdims=True)
    y = x * lax.rsqrt(ms + EPS) * gpre_ref[...]
    h = y * (1.0 + mod_ref[0, 1:2, :]) + mod_ref[0, 0:1, :]
    hb_ref[...] = h.astype(BF16)
    for q in range(d // 128):
        hf_ref[q] = h[:, q * 128:(q + 1) * 128]
    hp_refs = {GROUPS[1][1]: hp1_ref, GROUPS[2][1]: hp2_ref}
    for r, hp_ref in hp_refs.items():
        nl = tm // r
        for q in range(d // 128):
            for rho in range(r):
                v = hf_ref[q, pl.ds(rho, nl, stride=r), :]
                hp_ref[rho * nl:(rho + 1) * nl, q * 128:(q + 1) * 128] = v.astype(BF16)

    def proj(j, lhs_ref=hb_ref):
        return jnp.dot(lhs_ref[...], w_ref[:, j * CHUNK:(j + 1) * CHUNK],
                       preferred_element_type=F32)

    a_refs = (a0_ref, a1_ref, a2_ref)

    def tail(acc):
        return acc[tm - 8:tm, CHUNK - 128:CHUNK]

    def qkv_task(j):
        kind, g = divmod(j, N_GROUPS)
        r = GROUPS[g][1]
        if r == 1:
            acc = proj(j)
            a_refs[g][:, kind * ATT_W:(kind + 1) * ATT_W] = acc.astype(BF16)
        else:
            acc = proj(j, hp_refs[r])
            nl = tm // r
            for rho in range(r):
                c0 = (kind * r + rho) * ATT_W
                a_refs[g][:, c0:c0 + ATT_W] = acc[rho * nl:(rho + 1) * nl, :].astype(BF16)
        return tail(acc)

    def merge_task(t, j):
        acc = proj(j)
        rest_ref[:, t * CHUNK:(t + 1) * CHUNK] = _sigmoid(acc).astype(BF16)
        return tail(acc)

    def gatt_task():
        acc = proj(J_GATT)
        nm = len(J_MERGE)
        rest_ref[:, nm * CHUNK:(nm + 1) * CHUNK] = (acc * _sigmoid(acc)).astype(BF16)
        return tail(acc)

    simple = [functools.partial(merge_task, t, j) for t, j in enumerate(J_MERGE)] + [gatt_task]
    tasks = []
    for kind in range(3):
        tasks.append(functools.partial(qkv_task, kind * N_GROUPS))
    tasks += simple
    for g in (1, 2):
        for kind in range(3):
            tasks.append(functools.partial(qkv_task, kind * N_GROUPS + g))


    for t, j in enumerate(J_U):
        ubuf[8:8 + tm, t * CHUNK:(t + 1) * CHUNK] = proj(j)
    for _ in range(2):
        tasks.pop(0)()

    lb = c // 128
    cw = cw_ref[...]
    cb = cb_ref[...]
    rc = 64
    for k in range(tm // rc):
        base = 8 + k * rc
        acc = cb + cw[0:1, :] * ubuf[base:base + rc, :]
        for t in range(1, CONV_W):
            acc = acc + cw[t:t + 1, :] * ubuf[base - t:base - t + rc, :]
        ucb[k * rc:(k + 1) * rc, :] = acc.astype(BF16)
        for q in range(lb):
            uct[pl.ds(k * rc * lb + q, rc, stride=lb), :] = acc[:, q * 128:(q + 1) * 128]
    ubuf[0:8, :] = ubuf[tm:tm + 8, :]

    per = MXU_TILE // 128
    for k in range(c // MXU_TILE):
        cs = slice(k * MXU_TILE, (k + 1) * MXU_TILE)
        lhs = ucb[:, cs]
        rp = jnp.dot(lhs, wrg_ref[k], preferred_element_type=F32) + brg_ref[:, cs]
        ip = jnp.dot(lhs, wig_ref[k], preferred_element_type=F32) + big_ref[:, cs]
        for q in range(per):
            rows = pl.ds(k * per + q, tm, stride=lb)
            rpt[rows, :] = rp[:, q * 128:(q + 1) * 128]
            ipt[rows, :] = ip[:, q * 128:(q + 1) * 128]

    for t, j in enumerate(J_GLRU):
        acc = proj(j)
        glru[:, t * CHUNK:(t + 1) * CHUNK] = acc * _sigmoid(acc)

    z = -lam_ref[...]
    softplus = jnp.maximum(z, 0.0) + jnp.log1p(jnp.exp(-jnp.abs(z)))
    hn = jnp.concatenate([(-0.5 * LRU_C) * softplus] * SCAN_STEPS, axis=0)

    n_scan = tm // SCAN_STEPS
    n_tasks = len(tasks)
    issued = 0
    h = hcar[...]
    for g in range(n_scan):
        if issued < ((g + 1) * n_tasks) // n_scan:
            h = h + jnp.minimum(jnp.abs(tasks.pop(0)()), 0.0)
            issued += 1
        rows = slice(g * SCAN_STEPS * lb, (g + 1) * SCAN_STEPS * lb)
        log_a = jnp.tanh(rpt[rows, :]) * hn + hn
        a = jnp.exp(log_a)
        om = (-1.0 - a * a) * jnp.tanh(log_a)
        sq = jnp.where(om > 0.0, om * lax.rsqrt(om), 0.0)
        ig = 0.5 * jnp.tanh(ipt[rows, :]) + 0.5
        bb = (sq * uct[rows, :]) * ig
        hs = []
        for t in range(SCAN_STEPS):
            h = a[t * lb:(t + 1) * lb, :] * h + bb[t * lb:(t + 1) * lb, :]
            hs.append(h)
        rpt[rows, :] = jnp.concatenate(hs, axis=0)
        rs = slice(g * SCAN_STEPS, (g + 1) * SCAN_STEPS)
        for q in range(lb):
            cs = slice(q * 128, (q + 1) * 128)
            hv = rpt[pl.ds(g * SCAN_STEPS * lb + q, SCAN_STEPS, stride=lb), :]
            hbo_ref[rs, cs] = (hv * glru[rs, cs]).astype(BF16)
    hcar[...] = h
    for task in tasks:
        task()


def _inproj_lru_call(xf, mod, g_pre, w_in_b, conv_w, conv_b, wrg_bd, b_rg, wig_bd, b_ig,
                     lam, layer, tiles_per_batch, tm):
    n, d = xf.shape
    wn = w_in_b.shape[2]
    c = conv_w.shape[2]
    rest_w = (len(J_MERGE) + 1) * CHUNK
    nk = c // MXU_TILE
    row = lambda i: (i, 0)
    lay3 = lambda i: (layer, 0, 0)
    lay4 = lambda i: (layer, 0, 0, 0)
    return pl.pallas_call(
        functools.partial(_inproj_lru_kernel, tiles_per_batch=tiles_per_batch),
        grid=(n // tm,),
        in_specs=[
            pl.BlockSpec((tm, d), row),
            pl.BlockSpec((None, 1, 3, d), lambda i: (layer, i // tiles_per_batch, 0, 0)),
            pl.BlockSpec((None, 1, d), lay3),
            pl.BlockSpec((None, d, wn), lay3, pipeline_mode=pl.Buffered(1)),
            pl.BlockSpec((None, CONV_W, c), lay3),
            pl.BlockSpec((None, 1, c), lay3),
            pl.BlockSpec((None, nk, MXU_TILE, MXU_TILE), lay4),
            pl.BlockSpec((None, 1, c), lay3),
            pl.BlockSpec((None, nk, MXU_TILE, MXU_TILE), lay4),
            pl.BlockSpec((None, 1, c), lay3),
            pl.BlockSpec((None, c // 128, 128), lay3),
        ],
        out_specs=[
            pl.BlockSpec((tm // r, r * 3 * ATT_W), row) for _, r in GROUPS
        ] + [pl.BlockSpec((tm, rest_w), row), pl.BlockSpec((tm, c), row)],
        out_shape=[
            jax.ShapeDtypeStruct((n // r, r * 3 * ATT_W), BF16) for _, r in GROUPS
        ] + [jax.ShapeDtypeStruct((n, rest_w), BF16), jax.ShapeDtypeStruct((n, c), BF16)],
        scratch_shapes=[
            pltpu.VMEM((tm, d), BF16),
            pltpu.VMEM((d // 128, tm, 128), F32),
            pltpu.VMEM((tm, d), BF16),
            pltpu.VMEM((tm, d), BF16),
            pltpu.VMEM((tm + 8, c), F32),
            pltpu.VMEM((tm, c), BF16),
            pltpu.VMEM((tm * (c // 128), 128), F32),
            pltpu.VMEM((tm * (c // 128), 128), F32),
            pltpu.VMEM((tm * (c // 128), 128), F32),
            pltpu.VMEM((tm, c), F32),
            pltpu.VMEM((c // 128, 128), F32),
        ],
        compiler_params=pltpu.CompilerParams(
            dimension_semantics=("arbitrary",),
            vmem_limit_bytes=VMEM_LIMIT),
        name="inproj_lru",
    )(xf, mod, g_pre, w_in_b, conv_w, conv_b, wrg_bd, b_rg, wig_bd, b_ig, lam)


def _attn_kernel(q_ref, kp_ref, kc_ref, vp_ref, vc_ref, o_ref, st_ref, bias_ref,
                 *, tq, nres):
    row = lax.broadcasted_iota(jnp.int32, (BAND, 2 * BAND), 0)
    col = lax.broadcasted_iota(jnp.int32, (BAND, 2 * BAND), 1)
    band = (col >= row) & (col <= row + BAND)
    first = band & ((col >= BAND) | (pl.program_id(2) > 0))
    bias_ref[0] = jnp.where(band, 0.0, NEG)
    bias_ref[1] = jnp.where(first, 0.0, NEG)
    ones = jnp.ones((2 * BAND, HEAD_DIM), BF16)
    ln2 = 0.6931471805599453

    for rr in range(nres):
        for i in range(tq // BAND):
            rs = slice(i * BAND, (i + 1) * BAND)
            for h in range(HEADS):
                hs = slice(rr * ATT_W + h * HEAD_DIM, rr * ATT_W + (h + 1) * HEAD_DIM)
                q = q_ref[rs, hs]
                if i == 0:
                    kk = jnp.concatenate([kp_ref[:, hs], kc_ref[0:BAND, hs]], axis=0)
                    vv = jnp.concatenate([vp_ref[:, hs], vc_ref[0:BAND, hs]], axis=0)
                else:
                    kk = kc_ref[(i - 1) * BAND:(i + 1) * BAND, hs]
                    vv = vc_ref[(i - 1) * BAND:(i + 1) * BAND, hs]
                s = lax.dot_general(q, kk, (((1,), (1,)), ((), ())),
                                    preferred_element_type=F32)
                s = s + bias_ref[1 if i == 0 else 0]
                m = jnp.max(s, axis=-1, keepdims=True)
                p = jnp.exp2(s - m).astype(BF16)
                oe = jnp.dot(p, jnp.concatenate([vv, ones], axis=1),
                             preferred_element_type=F32)
                o_ref[rs, hs] = oe[:, 0:HEAD_DIM].astype(BF16)
                c0 = rr * 128 + 32 * h
                st_ref[rs, c0:c0 + 16] = jnp.broadcast_to(m * ln2, (BAND, 128))[:, 32 * h:32 * h + 16]
                st_ref[rs, c0 + 16:c0 + 32] = oe[:, HEAD_DIM + 32 * h + 16:HEAD_DIM + 32 * h + 32]


def _attn_call(a, g, r, batch, seq, tq, nres):
    sub = seq // r
    nblk = sub // tq
    per = tq // BAND
    nrb = r // nres

    def cur(kind):
        return lambda b, rb, i: (b * nblk + i, kind * nrb + rb)

    def prev(kind):
        return lambda b, rb, i: (b * nblk * per + jnp.maximum(i * per - 1, 0),
                                 kind * nrb + rb)

    out_map = lambda b, rb, i: (b * nblk + i, rb)
    return pl.pallas_call(
        functools.partial(_attn_kernel, tq=tq, nres=nres),
        grid=(batch, nrb, nblk),
        in_specs=[
            pl.BlockSpec((tq, nres * ATT_W), cur(0)),
            pl.BlockSpec((BAND, nres * ATT_W), prev(1)),
            pl.BlockSpec((tq, nres * ATT_W), cur(1)),
            pl.BlockSpec((BAND, nres * ATT_W), prev(2)),
            pl.BlockSpec((tq, nres * ATT_W), cur(2)),
        ],
        out_specs=[
            pl.BlockSpec((tq, nres * ATT_W), out_map),
            pl.BlockSpec((tq, nres * 128), out_map),
        ],
        out_shape=[
            jax.ShapeDtypeStruct((batch * sub, r * ATT_W), BF16),
            jax.ShapeDtypeStruct((batch * sub, r * 128), F32),
        ],
        scratch_shapes=[pltpu.VMEM((2, BAND, 2 * BAND), F32)],
        compiler_params=pltpu.CompilerParams(
            dimension_semantics=("parallel", "parallel", "parallel"),
            vmem_limit_bytes=VMEM_LIMIT),
        name=f"attn_g{g}",
    )(a, a, a, a, a)


def _post_kernel(x_ref, mod_ref, o0_ref, o1_ref, o2_ref, s0_ref, s1_ref, s2_ref,
                 gatt_ref, hb_ref, mrg_ref, wpa_ref, wpb_ref, wo_ref,
                 gpost_ref, out_ref, onat_ref, snat_ref):
    tm = x_ref.shape[0]
    d = x_ref.shape[1]
    for s0 in range(0, tm, POST_SUB):
        rs = slice(s0, s0 + POST_SUB)
        for g, (o_ref, s_ref) in enumerate(((o1_ref, s1_ref), (o2_ref, s2_ref))):
            r = GROUPS[g + 1][1]
            ls = slice(s0 // r, (s0 + POST_SUB) // r)
            for rho in range(r):
                rows = pl.ds(s0 + rho, POST_SUB // r, stride=r)
                snat_ref[g, rows, :] = s_ref[ls, rho * 128:(rho + 1) * 128]
                for h in range(HEADS):
                    c0 = rho * ATT_W + h * HEAD_DIM
                    onat_ref[g, h, rows, :] = o_ref[ls, c0:c0 + HEAD_DIM].astype(F32)
        ma, mb, mc = s0_ref[rs, :], snat_ref[0, rs, :], snat_ref[1, rs, :]
        la, lb, lc = (pltpu.roll(t, 128 - 16, axis=1) for t in (ma, mb, mc))
        mx = jnp.maximum(jnp.maximum(ma, mb), mc)
        ea, eb, ec = jnp.exp(ma - mx), jnp.exp(mb - mx), jnp.exp(mc - mx)
        inv = 1.0 / (ea * la + eb * lb + ec * lc)
        ws = (ea * inv, eb * inv, ec * inv)
        pieces = []
        for h in range(HEADS):
            hs = slice(h * HEAD_DIM, (h + 1) * HEAD_DIM)
            ogs = (o0_ref[rs, hs].astype(F32), onat_ref[0, h, rs, :], onat_ref[1, h, rs, :])
            acc = None
            for w, og in zip(ws, ogs):
                wb = jnp.broadcast_to(w[:, 32 * h:32 * h + 1], (POST_SUB, HEAD_DIM))
                t = wb * og
                acc = t if acc is None else acc + t
            pieces.append((acc * gatt_ref[rs, hs].astype(F32)).astype(BF16))
        ab = jnp.concatenate(pieces, axis=1)
        ya = jnp.dot(ab, wpa_ref[...], preferred_element_type=F32)
        yb = jnp.dot(hb_ref[rs, :], wpb_ref[...], preferred_element_type=F32)
        zz = mrg_ref[rs, 0:d].astype(F32) * ya + mrg_ref[rs, d:2 * d].astype(F32) * yb
        out = jnp.dot(zz.astype(BF16), wo_ref[...], preferred_element_type=F32)
        ms = jnp.mean(out * out, axis=-1, keepdims=True)
        r = out * lax.rsqrt(ms + EPS) * gpost_ref[...]
        out_ref[rs, :] = x_ref[rs, :] + mod_ref[0, 2:3, :] * r


def _post_call(xf, mod, os_, sts_, rest, hb, wpa_b, wpb_b, wo_b, g_post,
               layer, tiles_per_batch, tm):
    n, d = xf.shape
    row = lambda i: (i, 0)
    gatt_blk = lambda i: (i, (2 * d) // ATT_W)
    lay3 = lambda i: (layer, 0, 0)
    return pl.pallas_call(
        _post_kernel,
        grid=(n // tm,),
        in_specs=[
            pl.BlockSpec((tm, d), row),
            pl.BlockSpec((None, 1, 3, d), lambda i: (layer, i // tiles_per_batch, 0, 0)),
        ] + [pl.BlockSpec((tm // r, r * ATT_W), row) for _, r in GROUPS
        ] + [pl.BlockSpec((tm // r, r * 128), row) for _, r in GROUPS
        ] + [
            pl.BlockSpec((tm, ATT_W), gatt_blk),
            pl.BlockSpec((tm, d), row),
            pl.BlockSpec((tm, 2 * d), row),
            pl.BlockSpec((None, ATT_W, d), lay3, pipeline_mode=pl.Buffered(1)),
            pl.BlockSpec((None, d, d), lay3, pipeline_mode=pl.Buffered(1)),
            pl.BlockSpec((None, d, d), lay3, pipeline_mode=pl.Buffered(1)),
            pl.BlockSpec((None, 1, d), lay3),
        ],
        out_specs=pl.BlockSpec((tm, d), row),
        out_shape=jax.ShapeDtypeStruct((n, d), F32),
        scratch_shapes=[pltpu.VMEM((N_GROUPS - 1, HEADS, tm, HEAD_DIM), F32),
                        pltpu.VMEM((N_GROUPS - 1, tm, 128), F32)],
        compiler_params=pltpu.CompilerParams(
            dimension_semantics=("parallel",),
            vmem_limit_bytes=VMEM_LIMIT),
        name="post",
    )(xf, mod, *os_, *sts_, rest, hb, rest, wpa_b, wpb_b, wo_b, g_post)


def _block_diag(w):
    depth, nb, bd, _ = w.shape
    per = MXU_TILE // bd
    w4 = w.reshape(depth, nb // per, per, bd, bd)
    eye = jnp.eye(per, dtype=w.dtype)
    return jnp.einsum("lcghj,gk->lcghkj", w4, eye).reshape(depth, nb // per, MXU_TILE, MXU_TILE)


def kernel(x, c, w_mod, b_mod, g_pre, w_in, conv_w, conv_b, w_rg, b_rg, w_ig, b_ig,
           lru_lambda, w_pa, w_pb, w_o, g_post):
    batch, seq, d = x.shape
    depth = w_mod.shape[0]
    n = batch * seq
    tm_in, tm_post = 512, 1024

    mod = _mod_call(c.T, w_mod, b_mod).reshape(depth, batch, 3, d)
    row3 = lambda p: p.reshape(depth, 1, -1)
    qscale = jnp.where(jnp.arange(w_in.shape[2]) < N_GROUPS * ATT_W,
                       HEAD_DIM ** -0.5 * 1.4426950408889634, 1.0).astype(F32)
    w_in_b = (w_in * qscale).astype(BF16)
    wpa_b, wpb_b, wo_b = (w.astype(BF16) for w in (w_pa, w_pb, w_o))
    wrg_bd = (0.5 * _block_diag(w_rg)).astype(BF16)
    wig_bd = (0.5 * _block_diag(w_ig)).astype(BF16)
    lam = lru_lambda.reshape(depth, -1, 128)
    xf = x.reshape(n, d)
    for l in range(depth):
        *a_g, rest, hb = _inproj_lru_call(
            xf, mod, row3(g_pre), w_in_b, conv_w, row3(conv_b), wrg_bd, 0.5 * row3(b_rg),
            wig_bd, 0.5 * row3(b_ig), lam, l, seq // tm_in, tm_in)
        os_, sts_ = [], []
        for g, (_, r) in enumerate(GROUPS):
            o, st = _attn_call(a_g[g], g, r, batch, seq, *ATTN_TILES[g])
            os_.append(o)
            sts_.append(st)
        xf = _post_call(xf, mod, os_, sts_, rest, hb, wpa_b, wpb_b, wo_b, row3(g_post),
                        l, seq // tm_post, tm_post)
    return xf.reshape(batch, seq, d)
```

```python
import functools

import jax
import jax.numpy as jnp
from jax import lax
from jax.experimental import pallas as pl
from jax.experimental.pallas import tpu as pltpu

F32 = jnp.float32
BF16 = jnp.bfloat16

HEAD_DIM = 128
HEADS = 4
ATT_W = HEADS * HEAD_DIM
GROUPS = ((128, 1), (512, 4), (2048, 16))
N_GROUPS = 3
BAND = 128
LRU_BLOCKS = 16
CONV_W = 4
LRU_C = 8.0
EPS = 1e-6
NEG = -1e30

QKV_W = 3 * N_GROUPS * ATT_W
CHUNK = 512

VMEM_LIMIT = 58 * 1024 * 1024
MXU_TILE = 256


def _sigmoid(v):
    return 1.0 / (1.0 + jnp.exp(-v))


def _mod_kernel(ct_ref, w_ref, b_ref, o_ref):
    ct = ct_ref[...]
    sc = ct * _sigmoid(ct)
    w = w_ref[0]
    rows = []
    for b in range(ct.shape[1]):
        rows.append(jnp.sum(sc[:, b:b + 1] * w, axis=0, keepdims=True))
    o_ref[0] = jnp.concatenate(rows, axis=0) + b_ref[0]


def _mod_call(ct, w_mod, b_mod):
    depth, d, d3 = w_mod.shape
    nb = ct.shape[1]
    tn = d3
    return pl.pallas_call(
        _mod_kernel,
        grid=(depth, d3 // tn),
        in_specs=[
            pl.BlockSpec((d, nb), lambda l, j: (0, 0)),
            pl.BlockSpec((1, d, tn), lambda l, j: (l, 0, j)),
            pl.BlockSpec((1, 1, tn), lambda l, j: (l, 0, j)),
        ],
        out_specs=pl.BlockSpec((1, nb, tn), lambda l, j: (l, 0, j)),
        out_shape=jax.ShapeDtypeStruct((depth, nb, d3), F32),
        compiler_params=pltpu.CompilerParams(
            dimension_semantics=("parallel", "parallel"),
            vmem_limit_bytes=VMEM_LIMIT),
        name="mod",
    )(ct, w_mod, b_mod.reshape(depth, 1, d3))


J_GATT = 9
J_U = (10, 11)
J_GLRU = (12, 13)
J_MERGE = (14, 15, 16, 17)
ATTN_TILES = ((2048, 1), (2048, 1), (512, 4))
SCAN_STEPS = 16
POST_SUB = 256


def _inproj_lru_kernel(x_ref, mod_ref, gpre_ref, w_ref, cw_ref, cb_ref, wrg_ref, brg_ref,
                       wig_ref, big_ref, lam_ref,
                       a0_ref, a1_ref, a2_ref, rest_ref, hbo_ref,
                       hb_ref, hf_ref, hp1_ref, hp2_ref, ubuf, ucb, uct, rpt, ipt, glru, hcar,
                       *, tiles_per_batch):
    tm, d = x_ref.shape
    c = ucb.shape[1]

    @pl.when(pl.program_id(0) % tiles_per_batch == 0)
    def _():
        ubuf[0:8, :] = jnp.zeros((8, c), F32)
        hcar[...] = jnp.zeros(hcar.shape, F32)

    x = x_ref[...]
    ms = jnp.mean(x * x, axis=-1, keepdims=True)
    y = x * lax.rsqrt(ms + EPS) * gpre_ref[...]
    h = y * (1.0 + mod_ref[0, 1:2, :]) + mod_ref[0, 0:1, :]
    hb_ref[...] = h.astype(BF16)
    for q in range(d // 128):
        hf_ref[q] = h[:, q * 128:(q + 1) * 128]
    hp_refs = {GROUPS[1][1]: hp1_ref, GROUPS[2][1]: hp2_ref}
    for r, hp_ref in hp_refs.items():
        nl = tm // r
        for q in range(d // 128):
            for rho in range(r):
                v = hf_ref[q, pl.ds(rho, nl, stride=r), :]
                hp_ref[rho * nl:(rho + 1) * nl, q * 128:(q + 1) * 128] = v.astype(BF16)

    def proj(j, lhs_ref=hb_ref):
        return jnp.dot(lhs_ref[...], w_ref[:, j * CHUNK:(j + 1) * CHUNK],
                       preferred_element_type=F32)

    a_refs = (a0_ref, a1_ref, a2_ref)

    def tail(acc):
        return acc[tm - 8:tm, CHUNK - 128:CHUNK]

    def qkv_task(j):
        kind, g = divmod(j, N_GROUPS)
        r = GROUPS[g][1]
        if r == 1:
            acc = proj(j)
            a_refs[g][:, kind * ATT_W:(kind + 1) * ATT_W] = acc.astype(BF16)
        else:
            acc = proj(j, hp_refs[r])
            nl = tm // r
            for rho in range(r):
                c0 = (kind * r + rho) * ATT_W
                a_refs[g][:, c0:c0 + ATT_W] = acc[rho * nl:(rho + 1) * nl, :].astype(BF16)
        return tail(acc)

    def merge_task(t, j):
        acc = proj(j)
        rest_ref[:, t * CHUNK:(t + 1) * CHUNK] = _sigmoid(acc).astype(BF16)
        return tail(acc)

    def gatt_task():
        acc = proj(J_GATT)
        nm = len(J_MERGE)
        rest_ref[:, nm * CHUNK:(nm + 1) * CHUNK] = (acc * _sigmoid(acc)).astype(BF16)
        return tail(acc)

    simple = [functools.partial(merge_task, t, j) for t, j in enumerate(J_MERGE)] + [gatt_task]
    tasks = []
    for kind in range(3):
        tasks.append(functools.partial(qkv_task, kind * N_GROUPS))
    tasks += simple
    for g in (1, 2):
        for kind in range(3):
            tasks.append(functools.partial(qkv_task, kind * N_GROUPS + g))


    for t, j in enumerate(J_U):
        ubuf[8:8 + tm, t * CHUNK:(t + 1) * CHUNK] = proj(j)
    for _ in range(2):
        tasks.pop(0)()

    lb = c // 128
    cw = cw_ref[...]
    cb = cb_ref[...]
    rc = 64
    for k in range(tm // rc):
        base = 8 + k * rc
        acc = cb + cw[0:1, :] * ubuf[base:base + rc, :]
        for t in range(1, CONV_W):
            acc = acc + cw[t:t + 1, :] * ubuf[base - t:base - t + rc, :]
        ucb[k * rc:(k + 1) * rc, :] = acc.astype(BF16)
        for q in range(lb):
            uct[pl.ds(k * rc * lb + q, rc, stride=lb), :] = acc[:, q * 128:(q + 1) * 128]
    ubuf[0:8, :] = ubuf[tm:tm + 8, :]

    per = MXU_TILE // 128
    for k in range(c // MXU_TILE):
        cs = slice(k * MXU_TILE, (k + 1) * MXU_TILE)
        lhs = ucb[:, cs]
        rp = jnp.dot(lhs, wrg_ref[k], preferred_element_type=F32) + brg_ref[:, cs]
        ip = jnp.dot(lhs, wig_ref[k], preferred_element_type=F32) + big_ref[:, cs]
        for q in range(per):
            rows = pl.ds(k * per + q, tm, stride=lb)
            rpt[rows, :] = rp[:, q * 128:(q + 1) * 128]
            ipt[rows, :] = ip[:, q * 128:(q + 1) * 128]

    for t, j in enumerate(J_GLRU):
        acc = proj(j)
        glru[:, t * CHUNK:(t + 1) * CHUNK] = acc * _sigmoid(acc)

    z = -lam_ref[...]
    softplus = jnp.maximum(z, 0.0) + jnp.log1p(jnp.exp(-jnp.abs(z)))
    hn = jnp.concatenate([(-0.5 * LRU_C) * softplus] * SCAN_STEPS, axis=0)

    n_scan = tm // SCAN_STEPS
    n_tasks = len(tasks)
    issued = 0
    h = hcar[...]
    for g in range(n_scan):
        if issued < ((g + 1) * n_tasks) // n_scan:
            h = h + jnp.minimum(jnp.abs(tasks.pop(0)()), 0.0)
            issued += 1
        rows = slice(g * SCAN_STEPS * lb, (g + 1) * SCAN_STEPS * lb)
        log_a = jnp.tanh(rpt[rows, :]) * hn + hn
        a = jnp.exp(log_a)
        om = (-1.0 - a * a) * jnp.tanh(log_a)
        sq = jnp.where(om > 0.0, om * lax.rsqrt(om), 0.0)
        ig = 0.5 * jnp.tanh(ipt[rows, :]) + 0.5
        bb = (sq * uct[rows, :]) * ig
        hs = []
        for t in range(SCAN_STEPS):
            h = a[t * lb:(t + 1) * lb, :] * h + bb[t * lb:(t + 1) * lb, :]
            hs.append(h)
        rpt[rows, :] = jnp.concatenate(hs, axis=0)
        rs = slice(g * SCAN_STEPS, (g + 1) * SCAN_STEPS)
        for q in range(lb):
            cs = slice(q * 128, (q + 1) * 128)
            hv = rpt[pl.ds(g * SCAN_STEPS * lb + q, SCAN_STEPS, stride=lb), :]
            hbo_ref[rs, cs] = (hv * glru[rs, cs]).astype(BF16)
    hcar[...] = h
    for task in tasks:
        task()


def _inproj_lru_call(xf, mod, g_pre, w_in_b, conv_w, conv_b, wrg_bd, b_rg, wig_bd, b_ig,
                     lam, layer, tiles_per_batch, tm):
    n, d = xf.shape
    wn = w_in_b.shape[2]
    c = conv_w.shape[2]
    rest_w = (len(J_MERGE) + 1) * CHUNK
    nk = c // MXU_TILE
    row = lambda i: (i, 0)
    lay3 = lambda i: (layer, 0, 0)
    lay4 = lambda i: (layer, 0, 0, 0)
    return pl.pallas_call(
        functools.partial(_inproj_lru_kernel, tiles_per_batch=tiles_per_batch),
        grid=(n // tm,),
        in_specs=[
            pl.BlockSpec((tm, d), row),
            pl.BlockSpec((None, 1, 3, d), lambda i: (layer, i // tiles_per_batch, 0, 0)),
            pl.BlockSpec((None, 1, d), lay3),
            pl.BlockSpec((None, d, wn), lay3, pipeline_mode=pl.Buffered(1)),
            pl.BlockSpec((None, CONV_W, c), lay3),
            pl.BlockSpec((None, 1, c), lay3),
            pl.BlockSpec((None, nk, MXU_TILE, MXU_TILE), lay4),
            pl.BlockSpec((None, 1, c), lay3),
            pl.BlockSpec((None, nk, MXU_TILE, MXU_TILE), lay4),
            pl.BlockSpec((None, 1, c), lay3),
            pl.BlockSpec((None, c // 128, 128), lay3),
        ],
        out_specs=[
            pl.BlockSpec((tm // r, r * 3 * ATT_W), row) for _, r in GROUPS
        ] + [pl.BlockSpec((tm, rest_w), row), pl.BlockSpec((tm, c), row)],
        out_shape=[
            jax.ShapeDtypeStruct((n // r, r * 3 * ATT_W), BF16) for _, r in GROUPS
        ] + [jax.ShapeDtypeStruct((n, rest_w), BF16), jax.ShapeDtypeStruct((n, c), BF16)],
        scratch_shapes=[
            pltpu.VMEM((tm, d), BF16),
            pltpu.VMEM((d // 128, tm, 128), F32),
            pltpu.VMEM((tm, d), BF16),
            pltpu.VMEM((tm, d), BF16),
            pltpu.VMEM((tm + 8, c), F32),
            pltpu.VMEM((tm, c), BF16),
            pltpu.VMEM((tm * (c // 128), 128), F32),
            pltpu.VMEM((tm * (c // 128), 128), F32),
            pltpu.VMEM((tm * (c // 128), 128), F32),
            pltpu.VMEM((tm, c), F32),
            pltpu.VMEM((c // 128, 128), F32),
        ],
        compiler_params=pltpu.CompilerParams(
            dimension_semantics=("arbitrary",),
            vmem_limit_bytes=VMEM_LIMIT),
        name="inproj_lru",
    )(xf, mod, g_pre, w_in_b, conv_w, conv_b, wrg_bd, b_rg, wig_bd, b_ig, lam)


def _attn_kernel(q_ref, kp_ref, kc_ref, vp_ref, vc_ref, o_ref, st_ref, bias_ref,
                 *, tq, nres):
    row = lax.broadcasted_iota(jnp.int32, (BAND, 2 * BAND), 0)
    col = lax.broadcasted_iota(jnp.int32, (BAND, 2 * BAND), 1)
    band = (col >= row) & (col <= row + BAND)
    first = band & ((col >= BAND) | (pl.program_id(2) > 0))
    bias_ref[0] = jnp.where(band, 0.0, NEG)
    bias_ref[1] = jnp.where(first, 0.0, NEG)
    ones = jnp.ones((2 * BAND, HEAD_DIM), BF16)
    ln2 = 0.6931471805599453

    for rr in range(nres):
        for i in range(tq // BAND):
            rs = slice(i * BAND, (i + 1) * BAND)
            for h in range(HEADS):
                hs = slice(rr * ATT_W + h * HEAD_DIM, rr * ATT_W + (h + 1) * HEAD_DIM)
                q = q_ref[rs, hs]
                if i == 0:
                    kk = jnp.concatenate([kp_ref[:, hs], kc_ref[0:BAND, hs]], axis=0)
                    vv = jnp.concatenate([vp_ref[:, hs], vc_ref[0:BAND, hs]], axis=0)
                else:
                    kk = kc_ref[(i - 1) * BAND:(i + 1) * BAND, hs]
                    vv = vc_ref[(i - 1) * BAND:(i + 1) * BAND, hs]
                s = lax.dot_general(q, kk, (((1,), (1,)), ((), ())),
                                    preferred_element_type=F32)
                s = s + bias_ref[1 if i == 0 else 0]
                m = jnp.max(s, axis=-1, keepdims=True)
                p = jnp.exp2(s - m).astype(BF16)
                oe = jnp.dot(p, jnp.concatenate([vv, ones], axis=1),
                             preferred_element_type=F32)
                o_ref[rs, hs] = oe[:, 0:HEAD_DIM].astype(BF16)
                c0 = rr * 128 + 32 * h
                st_ref[rs, c0:c0 + 16] = jnp.broadcast_to(m * ln2, (BAND, 128))[:, 32 * h:32 * h + 16]
                st_ref[rs, c0 + 16:c0 + 32] = oe[:, HEAD_DIM + 32 * h + 16:HEAD_DIM + 32 * h + 32]


def _attn_call(a, g, r, batch, seq, tq, nres):
    sub = seq // r
    nblk = sub // tq
    per = tq // BAND
    nrb = r // nres

    def cur(kind):
        return lambda b, rb, i: (b * nblk + i, kind * nrb + rb)

    def prev(kind):
        return lambda b, rb, i: (b * nblk * per + jnp.maximum(i * per - 1, 0),
                                 kind * nrb + rb)

    out_map = lambda b, rb, i: (b * nblk + i, rb)
    return pl.pallas_call(
        functools.partial(_attn_kernel, tq=tq, nres=nres),
        grid=(batch, nrb, nblk),
        in_specs=[
            pl.BlockSpec((tq, nres * ATT_W), cur(0)),
            pl.BlockSpec((BAND, nres * ATT_W), prev(1)),
            pl.BlockSpec((tq, nres * ATT_W), cur(1)),
            pl.BlockSpec((BAND, nres * ATT_W), prev(2)),
            pl.BlockSpec((tq, nres * ATT_W), cur(2)),
        ],
        out_specs=[
            pl.BlockSpec((tq, nres * ATT_W), out_map),
            pl.BlockSpec((tq, nres * 128), out_map),
        ],
        out_shape=[
            jax.ShapeDtypeStruct((batch * sub, r * ATT_W), BF16),
            jax.ShapeDtypeStruct((batch * sub, r * 128), F32),
        ],
        scratch_shapes=[pltpu.VMEM((2, BAND, 2 * BAND), F32)],
        compiler_params=pltpu.CompilerParams(
            dimension_semantics=("parallel", "parallel", "parallel"),
            vmem_limit_bytes=VMEM_LIMIT),
        name=f"attn_g{g}",
    )(a, a, a, a, a)


def _post_kernel(x_ref, mod_ref, o0_ref, o1_ref, o2_ref, s0_ref, s1_ref, s2_ref,
                 gatt_ref, hb_ref, mrg_ref, wpa_ref, wpb_ref, wo_ref,
                 gpost_ref, out_ref, onat_ref, snat_ref):
    tm = x_ref.shape[0]
    d = x_ref.shape[1]
    for s0 in range(0, tm, POST_SUB):
        rs = slice(s0, s0 + POST_SUB)
        for g, (o_ref, s_ref) in enumerate(((o1_ref, s1_ref), (o2_ref, s2_ref))):
            r = GROUPS[g + 1][1]
            ls = slice(s0 // r, (s0 + POST_SUB) // r)
            for rho in range(r):
                rows = pl.ds(s0 + rho, POST_SUB // r, stride=r)
                snat_ref[g, rows, :] = s_ref[ls, rho * 128:(rho + 1) * 128]
                for h in range(HEADS):
                    c0 = rho * ATT_W + h * HEAD_DIM
                    onat_ref[g, h, rows, :] = o_ref[ls, c0:c0 + HEAD_DIM].astype(F32)
        ma, mb, mc = s0_ref[rs, :], snat_ref[0, rs, :], snat_ref[1, rs, :]
        la, lb, lc = (pltpu.roll(t, 128 - 16, axis=1) for t in (ma, mb, mc))
        mx = jnp.maximum(jnp.maximum(ma, mb), mc)
        ea, eb, ec = jnp.exp(ma - mx), jnp.exp(mb - mx), jnp.exp(mc - mx)
        inv = 1.0 / (ea * la + eb * lb + ec * lc)
        ws = (ea * inv, eb * inv, ec * inv)
        pieces = []
        for h in range(HEADS):
            hs = slice(h * HEAD_DIM, (h + 1) * HEAD_DIM)
            ogs = (o0_ref[rs, hs].astype(F32), onat_ref[0, h, rs, :], onat_ref[1, h, rs, :])
            acc = None
            for w, og in zip(ws, ogs):
                wb = jnp.broadcast_to(w[:, 32 * h:32 * h + 1], (POST_SUB, HEAD_DIM))
                t = wb * og
                acc = t if acc is None else acc + t
            pieces.append((acc * gatt_ref[rs, hs].astype(F32)).astype(BF16))
        ab = jnp.concatenate(pieces, axis=1)
        ya = jnp.dot(ab, wpa_ref[...], preferred_element_type=F32)
        yb = jnp.dot(hb_ref[rs, :], wpb_ref[...], preferred_element_type=F32)
        zz = mrg_ref[rs, 0:d].astype(F32) * ya + mrg_ref[rs, d:2 * d].astype(F32) * yb
        out = jnp.dot(zz.astype(BF16), wo_ref[...], preferred_element_type=F32)
        ms = jnp.mean(out * out, axis=-1, keepdims=True)
        r = out * lax.rsqrt(ms + EPS) * gpost_ref[...]
        out_ref[rs, :] = x_ref[rs, :] + mod_ref[0, 2:3, :] * r


def _post_call(xf, mod, os_, sts_, rest, hb, wpa_b, wpb_b, wo_b, g_post,
               layer, tiles_per_batch, tm):
    n, d = xf.shape
    row = lambda i: (i, 0)
    gatt_blk = lambda i: (i, (2 * d) // ATT_W)
    lay3 = lambda i: (layer, 0, 0)
    return pl.pallas_call(
        _post_kernel,
        grid=(n // tm,),
        in_specs=[
            pl.BlockSpec((tm, d), row),
            pl.BlockSpec((None, 1, 3, d), lambda i: (layer, i // tiles_per_batch, 0, 0)),
        ] + [pl.BlockSpec((tm // r, r * ATT_W), row) for _, r in GROUPS
        ] + [pl.BlockSpec((tm // r, r * 128), row) for _, r in GROUPS
        ] + [
            pl.BlockSpec((tm, ATT_W), gatt_blk),
            pl.BlockSpec((tm, d), row),
            pl.BlockSpec((tm, 2 * d), row),
            pl.BlockSpec((None, ATT_W, d), lay3, pipeline_mode=pl.Buffered(1)),
            pl.BlockSpec((None, d, d), lay3, pipeline_mode=pl.Buffered(1)),
            pl.BlockSpec((None, d, d), lay3, pipeline_mode=pl.Buffered(1)),
            pl.BlockSpec((None, 1, d), lay3),
        ],
        out_specs=pl.BlockSpec((tm, d), row),
        out_shape=jax.ShapeDtypeStruct((n, d), F32),
        scratch_shapes=[pltpu.VMEM((N_GROUPS - 1, HEADS, tm, HEAD_DIM), F32),
                        pltpu.VMEM((N_GROUPS - 1, tm, 128), F32)],
        compiler_params=pltpu.CompilerParams(
            dimension_semantics=("parallel",),
            vmem_limit_bytes=VMEM_LIMIT),
        name="post",
    )(xf, mod, *os_, *sts_, rest, hb, rest, wpa_b, wpb_b, wo_b, g_post)


def _block_diag(w):
    depth, nb, bd, _ = w.shape
    per = MXU_TILE // bd
    w4 = w.reshape(depth, nb // per, per, bd, bd)
    eye = jnp.eye(per, dtype=w.dtype)
    return jnp.einsum("lcghj,gk->lcghkj", w4, eye).reshape(depth, nb // per, MXU_TILE, MXU_TILE)


def kernel(x, c, w_mod, b_mod, g_pre, w_in, conv_w, conv_b, w_rg, b_rg, w_ig, b_ig,
           lru_lambda, w_pa, w_pb, w_o, g_post):
    batch, seq, d = x.shape
    depth = w_mod.shape[0]
    n = batch * seq
    tm_in, tm_post = 512, 1024

    mod = _mod_call(c.T, w_mod, b_mod).reshape(depth, batch, 3, d)
    row3 = lambda p: p.reshape(depth, 1, -1)
    qscale = jnp.where(jnp.arange(w_in.shape[2]) < N_GROUPS * ATT_W,
                       HEAD_DIM ** -0.5 * 1.4426950408889634, 1.0).astype(F32)
    w_in_b = (w_in * qscale).astype(BF16)
    wpa_b, wpb_b, wo_b = (w.astype(BF16) for w in (w_pa, w_pb, w_o))
    wrg_bd = (0.5 * _block_diag(w_rg)).astype(BF16)
    wig_bd = (0.5 * _block_diag(w_ig)).astype(BF16)
    lam = lru_lambda.reshape(depth, -1, 128)
    xf = x.reshape(n, d)
    for l in range(depth):
        *a_g, rest, hb = _inproj_lru_call(
            xf, mod, row3(g_pre), w_in_b, conv_w, row3(conv_b), wrg_bd, 0.5 * row3(b_rg),
            wig_bd, 0.5 * row3(b_ig), lam, l, seq // tm_in, tm_in)
        os_, sts_ = [], []
        for g, (_, r) in enumerate(GROUPS):
            o, st = _attn_call(a_g[g], g, r, batch, seq, *ATTN_TILES[g])
            os_.append(o)
            sts_.append(st)
        xf = _post_call(xf, mod, os_, sts_, rest, hb, wpa_b, wpb_b, wo_b, row3(g_post),
                        l, seq // tm_post, tm_post)
    return xf.reshape(batch, seq, d)
```

```python
import functools

import jax
import jax.numpy as jnp
from jax import lax
from jax.experimental import pallas as pl
from jax.experimental.pallas import tpu as pltpu

F32 = jnp.float32
BF16 = jnp.bfloat16

HEAD_DIM = 128
HEADS = 4
ATT_W = HEADS * HEAD_DIM
GROUPS = ((128, 1), (512, 4), (2048, 16))
N_GROUPS = 3
BAND = 128
LRU_BLOCKS = 16
CONV_W = 4
LRU_C = 8.0
EPS = 1e-6
NEG = -1e30

QKV_W = 3 * N_GROUPS * ATT_W
CHUNK = 512

VMEM_LIMIT = 58 * 1024 * 1024
MXU_TILE = 256


def _sigmoid(v):
    return 1.0 / (1.0 + jnp.exp(-v))


def _mod_kernel(ct_ref, w_ref, b_ref, o_ref):
    ct = ct_ref[...]
    sc = ct * _sigmoid(ct)
    w = w_ref[0]
    rows = []
    for b in range(ct.shape[1]):
        rows.append(jnp.sum(sc[:, b:b + 1] * w, axis=0, keepdims=True))
    o_ref[0] = jnp.concatenate(rows, axis=0) + b_ref[0]


def _mod_call(ct, w_mod, b_mod):
    depth, d, d3 = w_mod.shape
    nb = ct.shape[1]
    tn = 1024
    return pl.pallas_call(
        _mod_kernel,
        grid=(depth, d3 // tn),
        in_specs=[
            pl.BlockSpec((d, nb), lambda l, j: (0, 0)),
            pl.BlockSpec((1, d, tn), lambda l, j: (l, 0, j)),
            pl.BlockSpec((1, 1, tn), lambda l, j: (l, 0, j)),
        ],
        out_specs=pl.BlockSpec((1, nb, tn), lambda l, j: (l, 0, j)),
        out_shape=jax.ShapeDtypeStruct((depth, nb, d3), F32),
        compiler_params=pltpu.CompilerParams(
            dimension_semantics=("parallel", "parallel"),
            vmem_limit_bytes=VMEM_LIMIT),
        name="mod",
    )(ct, w_mod, b_mod.reshape(depth, 1, d3))


J_GATT = 9
J_U = (10, 11)
J_GLRU = (12, 13)
J_MERGE = (14, 15, 16, 17)
ATTN_TILES = ((2048, 1), (2048, 1), (512, 4))
SCAN_STEPS = 16
POST_SUB = 256


def _inproj_lru_kernel(x_ref, mod_ref, gpre_ref, w_ref, cw_ref, cb_ref, wrg_ref, brg_ref,
                       wig_ref, big_ref, lam_ref,
                       a0_ref, a1_ref, a2_ref, rest_ref, hbo_ref,
                       hb_ref, hf_ref, hp1_ref, hp2_ref, ubuf, ucb, uct, rpt, ipt, glru, hcar,
                       *, tiles_per_batch):
    tm, d = x_ref.shape
    c = ucb.shape[1]
    lb = c // 128

    @pl.when(pl.program_id(0) % tiles_per_batch == 0)
    def _():
        ubuf[0:8, :] = jnp.zeros((8, c), F32)
        hcar[...] = jnp.zeros(hcar.shape, F32)

    x = x_ref[...]
    ms = jnp.mean(x * x, axis=-1, keepdims=True)
    y = x * lax.rsqrt(ms + EPS) * gpre_ref[...]
    h = y * (1.0 + mod_ref[0, 1:2, :]) + mod_ref[0, 0:1, :]
    hb_ref[...] = h.astype(BF16)
    for q in range(d // 128):
        hf_ref[q] = h[:, q * 128:(q + 1) * 128]
    hp_refs = {GROUPS[1][1]: hp1_ref, GROUPS[2][1]: hp2_ref}
    for r, hp_ref in hp_refs.items():
        nl = tm // r
        for q in range(d // 128):
            for rho in range(r):
                v = hf_ref[q, pl.ds(rho, nl, stride=r), :]
                hp_ref[rho * nl:(rho + 1) * nl, q * 128:(q + 1) * 128] = v.astype(BF16)

    def proj(j, lhs_ref=hb_ref):
        return jnp.dot(lhs_ref[...], w_ref[j], preferred_element_type=F32)

    def anchor(acc):
        return jnp.minimum(jnp.abs(acc[tm - 8:tm, CHUNK - 128:CHUNK]), 0.0)

    a_refs = (a0_ref, a1_ref, a2_ref)

    def qkv_chunk(kind, g):
        r = GROUPS[g][1]
        j = kind * N_GROUPS + g
        if r == 1:
            acc = proj(j)
            a_refs[g][kind] = acc.astype(BF16)
        else:
            acc = proj(j, hp_refs[r])
            nl = tm // r
            for rho in range(r):
                a_refs[g][kind, :, rho * ATT_W:(rho + 1) * ATT_W] = (
                    acc[rho * nl:(rho + 1) * nl, :].astype(BF16))
        return anchor(acc)

    def merge_chunk(t):
        acc = proj(J_MERGE[0] + t)
        rest_ref[t] = _sigmoid(acc).astype(BF16)
        return anchor(acc)


    for t, j in enumerate(J_U):
        ubuf[8:8 + tm, t * CHUNK:(t + 1) * CHUNK] = proj(j)
    n_loop = len(J_MERGE) - 1
    merge_chunk(n_loop)
    acc = proj(J_GATT)
    rest_ref[len(J_MERGE)] = (acc * _sigmoid(acc)).astype(BF16)

    cw = cw_ref[...]
    cb = cb_ref[...]
    rc = 64
    for k in range(tm // rc):
        base = 8 + k * rc
        acc = cb + cw[0:1, :] * ubuf[base:base + rc, :]
        for t in range(1, CONV_W):
            acc = acc + cw[t:t + 1, :] * ubuf[base - t:base - t + rc, :]
        ucb[k * rc:(k + 1) * rc, :] = acc.astype(BF16)
        for q in range(lb):
            uct[pl.ds(k * rc * lb + q, rc, stride=lb), :] = acc[:, q * 128:(q + 1) * 128]
    ubuf[0:8, :] = ubuf[tm:tm + 8, :]

    per = MXU_TILE // 128
    for k in range(c // MXU_TILE):
        cs = slice(k * MXU_TILE, (k + 1) * MXU_TILE)
        lhs = ucb[:, cs]
        rp = jnp.dot(lhs, wrg_ref[k], preferred_element_type=F32) + brg_ref[:, cs]
        ip = jnp.dot(lhs, wig_ref[k], preferred_element_type=F32) + big_ref[:, cs]
        for q in range(per):
            rows = pl.ds(k * per + q, tm, stride=lb)
            rpt[rows, :] = rp[:, q * 128:(q + 1) * 128]
            ipt[rows, :] = ip[:, q * 128:(q + 1) * 128]

    for t, j in enumerate(J_GLRU):
        acc = proj(j)
        glru[:, t * CHUNK:(t + 1) * CHUNK] = acc * _sigmoid(acc)

    z = -lam_ref[...]
    softplus = jnp.maximum(z, 0.0) + jnp.log1p(jnp.exp(-jnp.abs(z)))
    hn = jnp.concatenate([(-0.5 * LRU_C) * softplus] * SCAN_STEPS, axis=0)

    def scan_group(g, h):
        r0 = pl.multiple_of(g * (SCAN_STEPS * lb), SCAN_STEPS * lb)
        rows = pl.ds(r0, SCAN_STEPS * lb)
        log_a = jnp.tanh(rpt[rows, :]) * hn + hn
        a = jnp.exp(log_a)
        om = (-1.0 - a * a) * jnp.tanh(log_a)
        sq = jnp.where(om > 0.0, om * lax.rsqrt(om), 0.0)
        ig = 0.5 * jnp.tanh(ipt[rows, :]) + 0.5
        bb = (sq * uct[rows, :]) * ig
        hs = []
        for t in range(SCAN_STEPS):
            h = a[t * lb:(t + 1) * lb, :] * h + bb[t * lb:(t + 1) * lb, :]
            hs.append(h)
        rpt[rows, :] = jnp.concatenate(hs, axis=0)
        t0 = pl.multiple_of(g * SCAN_STEPS, SCAN_STEPS)
        rs = pl.ds(t0, SCAN_STEPS)
        for q in range(lb):
            cs = slice(q * 128, (q + 1) * 128)
            hv = rpt[pl.ds(r0 + q, SCAN_STEPS, stride=lb), :]
            hbo_ref[rs, cs] = (hv * glru[rs, cs]).astype(BF16)
        return h

    n_scan = tm // SCAN_STEPS
    per_trip = n_scan // n_loop
    per_slot = (per_trip + 3) // 4

    def trip(it, h):
        g0 = it * per_trip
        done = 0
        for slot in range(4):
            if slot < N_GROUPS:
                h = h + qkv_chunk(it, slot)
            else:
                h = h + merge_chunk(it)
            for _ in range(min(per_slot, per_trip - done)):
                h = scan_group(g0 + done, h)
                done += 1
        return h

    h = lax.fori_loop(0, n_loop, trip, hcar[...])
    for g in range(n_loop * per_trip, n_scan):
        h = scan_group(g, h)
    hcar[...] = h


def _inproj_lru_call(xf, mod, g_pre, w_in_b, conv_w, conv_b, wrg_bd, b_rg, wig_bd, b_ig,
                     lam, layer, tiles_per_batch, tm):
    n, d = xf.shape
    nchunk = w_in_b.shape[1]
    c = conv_w.shape[2]
    n_rest = len(J_MERGE) + 1
    nk = c // MXU_TILE
    row = lambda i: (i, 0)
    lay3 = lambda i: (layer, 0, 0)
    lay4 = lambda i: (layer, 0, 0, 0)
    return pl.pallas_call(
        functools.partial(_inproj_lru_kernel, tiles_per_batch=tiles_per_batch),
        grid=(n // tm,),
        in_specs=[
            pl.BlockSpec((tm, d), row),
            pl.BlockSpec((None, 1, 3, d), lambda i: (layer, i // tiles_per_batch, 0, 0)),
            pl.BlockSpec((None, 1, d), lay3),
            pl.BlockSpec((None, nchunk, d, CHUNK), lay4, pipeline_mode=pl.Buffered(1)),
            pl.BlockSpec((None, CONV_W, c), lay3),
            pl.BlockSpec((None, 1, c), lay3),
            pl.BlockSpec((None, nk, MXU_TILE, MXU_TILE), lay4),
            pl.BlockSpec((None, 1, c), lay3),
            pl.BlockSpec((None, nk, MXU_TILE, MXU_TILE), lay4),
            pl.BlockSpec((None, 1, c), lay3),
            pl.BlockSpec((None, c // 128, 128), lay3),
        ],
        out_specs=[
            pl.BlockSpec((3, tm // r, r * ATT_W), lambda i: (0, i, 0)) for _, r in GROUPS
        ] + [pl.BlockSpec((n_rest, tm, CHUNK), lambda i: (0, i, 0)),
             pl.BlockSpec((tm, c), row)],
        out_shape=[
            jax.ShapeDtypeStruct((3, n // r, r * ATT_W), BF16) for _, r in GROUPS
        ] + [jax.ShapeDtypeStruct((n_rest, n, CHUNK), BF16),
             jax.ShapeDtypeStruct((n, c), BF16)],
        scratch_shapes=[
            pltpu.VMEM((tm, d), BF16),
            pltpu.VMEM((d // 128, tm, 128), F32),
            pltpu.VMEM((tm, d), BF16),
            pltpu.VMEM((tm, d), BF16),
            pltpu.VMEM((tm + 8, c), F32),
            pltpu.VMEM((tm, c), BF16),
            pltpu.VMEM((tm * (c // 128), 128), F32),
            pltpu.VMEM((tm * (c // 128), 128), F32),
            pltpu.VMEM((tm * (c // 128), 128), F32),
            pltpu.VMEM((tm, c), F32),
            pltpu.VMEM((c // 128, 128), F32),
        ],
        compiler_params=pltpu.CompilerParams(
            dimension_semantics=("arbitrary",),
            vmem_limit_bytes=VMEM_LIMIT),
        name="inproj_lru",
    )(xf, mod, g_pre, w_in_b, conv_w, conv_b, wrg_bd, b_rg, wig_bd, b_ig, lam)


def _attn_kernel(q_ref, kp_ref, kc_ref, vp_ref, vc_ref, o_ref, st_ref, bias_ref,
                 *, tq, nres):
    row = lax.broadcasted_iota(jnp.int32, (BAND, 2 * BAND), 0)
    col = lax.broadcasted_iota(jnp.int32, (BAND, 2 * BAND), 1)
    band = (col >= row) & (col <= row + BAND)
    first = band & ((col >= BAND) | (pl.program_id(2) > 0))
    bias_ref[0] = jnp.where(band, 0.0, NEG)
    bias_ref[1] = jnp.where(first, 0.0, NEG)
    ones = jnp.ones((2 * BAND, HEAD_DIM), BF16)
    ln2 = 0.6931471805599453

    for rr in range(nres):
        for i in range(tq // BAND):
            rs = slice(i * BAND, (i + 1) * BAND)
            for h in range(HEADS):
                hs = slice(rr * ATT_W + h * HEAD_DIM, rr * ATT_W + (h + 1) * HEAD_DIM)
                q = q_ref[rs, hs]
                if i == 0:
                    kk = jnp.concatenate([kp_ref[:, hs], kc_ref[0:BAND, hs]], axis=0)
                    vv = jnp.concatenate([vp_ref[:, hs], vc_ref[0:BAND, hs]], axis=0)
                else:
                    kk = kc_ref[(i - 1) * BAND:(i + 1) * BAND, hs]
                    vv = vc_ref[(i - 1) * BAND:(i + 1) * BAND, hs]
                s = lax.dot_general(q, kk, (((1,), (1,)), ((), ())),
                                    preferred_element_type=F32)
                s = s + bias_ref[1 if i == 0 else 0]
                m = jnp.max(s, axis=-1, keepdims=True)
                p = jnp.exp2(s - m).astype(BF16)
                oe = jnp.dot(p, jnp.concatenate([vv, ones], axis=1),
                             preferred_element_type=F32)
                o_ref[rs, hs] = oe[:, 0:HEAD_DIM].astype(BF16)
                c0 = rr * 128 + 32 * h
                st_ref[rs, c0:c0 + 16] = jnp.broadcast_to(m * ln2, (BAND, 128))[:, 32 * h:32 * h + 16]
                st_ref[rs, c0 + 16:c0 + 32] = oe[:, HEAD_DIM + 32 * h + 16:HEAD_DIM + 32 * h + 32]


def _attn_call(a, g, r, batch, seq, tq, nres):
    sub = seq // r
    nblk = sub // tq
    per = tq // BAND
    nrb = r // nres

    def cur(kind):
        return lambda b, rb, i: (kind, b * nblk + i, rb)

    def prev(kind):
        return lambda b, rb, i: (kind, b * nblk * per + jnp.maximum(i * per - 1, 0), rb)

    out_map = lambda b, rb, i: (b * nblk + i, rb)
    return pl.pallas_call(
        functools.partial(_attn_kernel, tq=tq, nres=nres),
        grid=(batch, nrb, nblk),
        in_specs=[
            pl.BlockSpec((None, tq, nres * ATT_W), cur(0)),
            pl.BlockSpec((None, BAND, nres * ATT_W), prev(1)),
            pl.BlockSpec((None, tq, nres * ATT_W), cur(1)),
            pl.BlockSpec((None, BAND, nres * ATT_W), prev(2)),
            pl.BlockSpec((None, tq, nres * ATT_W), cur(2)),
        ],
        out_specs=[
            pl.BlockSpec((tq, nres * ATT_W), out_map),
            pl.BlockSpec((tq, nres * 128), out_map),
        ],
        out_shape=[
            jax.ShapeDtypeStruct((batch * sub, r * ATT_W), BF16),
            jax.ShapeDtypeStruct((batch * sub, r * 128), F32),
        ],
        scratch_shapes=[pltpu.VMEM((2, BAND, 2 * BAND), F32)],
        compiler_params=pltpu.CompilerParams(
            dimension_semantics=("parallel", "parallel", "parallel"),
            vmem_limit_bytes=VMEM_LIMIT),
        name=f"attn_g{g}",
    )(a, a, a, a, a)


def _post_kernel(x_ref, mod_ref, o0_ref, o1_ref, o2_ref, s0_ref, s1_ref, s2_ref,
                 gatt_ref, hb_ref, mrg_ref, wpa_ref, wpb_ref, wo_ref,
                 gpost_ref, out_ref, onat_ref, snat_ref):
    tm = x_ref.shape[0]
    d = x_ref.shape[1]
    for s0 in range(0, tm, POST_SUB):
        rs = slice(s0, s0 + POST_SUB)
        for g, (o_ref, s_ref) in enumerate(((o1_ref, s1_ref), (o2_ref, s2_ref))):
            r = GROUPS[g + 1][1]
            ls = slice(s0 // r, (s0 + POST_SUB) // r)
            for rho in range(r):
                rows = pl.ds(s0 + rho, POST_SUB // r, stride=r)
                snat_ref[g, rows, :] = s_ref[ls, rho * 128:(rho + 1) * 128]
                for h in range(HEADS):
                    c0 = rho * ATT_W + h * HEAD_DIM
                    onat_ref[g, h, rows, :] = o_ref[ls, c0:c0 + HEAD_DIM].astype(F32)
        ma, mb, mc = s0_ref[rs, :], snat_ref[0, rs, :], snat_ref[1, rs, :]
        la, lb, lc = (pltpu.roll(t, 128 - 16, axis=1) for t in (ma, mb, mc))
        mx = jnp.maximum(jnp.maximum(ma, mb), mc)
        ea, eb, ec = jnp.exp(ma - mx), jnp.exp(mb - mx), jnp.exp(mc - mx)
        inv = 1.0 / (ea * la + eb * lb + ec * lc)
        ws = (ea * inv, eb * inv, ec * inv)
        pieces = []
        for h in range(HEADS):
            hs = slice(h * HEAD_DIM, (h + 1) * HEAD_DIM)
            ogs = (o0_ref[rs, hs].astype(F32), onat_ref[0, h, rs, :], onat_ref[1, h, rs, :])
            acc = None
            for w, og in zip(ws, ogs):
                wb = jnp.broadcast_to(w[:, 32 * h:32 * h + 1], (POST_SUB, HEAD_DIM))
                t = wb * og
                acc = t if acc is None else acc + t
            pieces.append((acc * gatt_ref[rs, hs].astype(F32)).astype(BF16))
        ab = jnp.concatenate(pieces, axis=1)
        ya = jnp.dot(ab, wpa_ref[...], preferred_element_type=F32)
        yb = jnp.dot(hb_ref[rs, :], wpb_ref[...], preferred_element_type=F32)
        half = mrg_ref.shape[0] // 2
        ga = jnp.concatenate([mrg_ref[t, rs, :] for t in range(half)], axis=1)
        gb = jnp.concatenate([mrg_ref[half + t, rs, :] for t in range(half)], axis=1)
        zz = ga.astype(F32) * ya + gb.astype(F32) * yb
        out = jnp.dot(zz.astype(BF16), wo_ref[...], preferred_element_type=F32)
        ms = jnp.mean(out * out, axis=-1, keepdims=True)
        r = out * lax.rsqrt(ms + EPS) * gpost_ref[...]
        out_ref[rs, :] = x_ref[rs, :] + mod_ref[0, 2:3, :] * r


def _post_call(xf, mod, os_, sts_, rest, hb, wpa_b, wpb_b, wo_b, g_post,
               layer, tiles_per_batch, tm):
    n, d = xf.shape
    row = lambda i: (i, 0)
    n_mrg = len(J_MERGE)
    lay3 = lambda i: (layer, 0, 0)
    return pl.pallas_call(
        _post_kernel,
        grid=(n // tm,),
        in_specs=[
            pl.BlockSpec((tm, d), row),
            pl.BlockSpec((None, 1, 3, d), lambda i: (layer, i // tiles_per_batch, 0, 0)),
        ] + [pl.BlockSpec((tm // r, r * ATT_W), row) for _, r in GROUPS
        ] + [pl.BlockSpec((tm // r, r * 128), row) for _, r in GROUPS
        ] + [
            pl.BlockSpec((None, tm, CHUNK), lambda i: (n_mrg, i, 0)),
            pl.BlockSpec((tm, d), row),
            pl.BlockSpec((n_mrg, tm, CHUNK), lambda i: (0, i, 0)),
            pl.BlockSpec((None, ATT_W, d), lay3, pipeline_mode=pl.Buffered(1)),
            pl.BlockSpec((None, d, d), lay3, pipeline_mode=pl.Buffered(1)),
            pl.BlockSpec((None, d, d), lay3, pipeline_mode=pl.Buffered(1)),
            pl.BlockSpec((None, 1, d), lay3),
        ],
        out_specs=pl.BlockSpec((tm, d), row),
        out_shape=jax.ShapeDtypeStruct((n, d), F32),
        scratch_shapes=[pltpu.VMEM((N_GROUPS - 1, HEADS, tm, HEAD_DIM), F32),
                        pltpu.VMEM((N_GROUPS - 1, tm, 128), F32)],
        compiler_params=pltpu.CompilerParams(
            dimension_semantics=("parallel",),
            vmem_limit_bytes=VMEM_LIMIT),
        name="post",
    )(xf, mod, *os_, *sts_, rest, hb, rest, wpa_b, wpb_b, wo_b, g_post)


def _block_diag(w):
    depth, nb, bd, _ = w.shape
    per = MXU_TILE // bd
    w4 = w.reshape(depth, nb // per, per, bd, bd)
    eye = jnp.eye(per, dtype=w.dtype)
    return jnp.einsum("lcghj,gk->lcghkj", w4, eye).reshape(depth, nb // per, MXU_TILE, MXU_TILE)


def kernel(x, c, w_mod, b_mod, g_pre, w_in, conv_w, conv_b, w_rg, b_rg, w_ig, b_ig,
           lru_lambda, w_pa, w_pb, w_o, g_post):
    batch, seq, d = x.shape
    depth = w_mod.shape[0]
    n = batch * seq
    tm_in, tm_post = 512, 1024

    mod = _mod_call(c.T, w_mod, b_mod).reshape(depth, batch, 3, d)
    row3 = lambda p: p.reshape(depth, 1, -1)
    qscale = jnp.where(jnp.arange(w_in.shape[2]) < N_GROUPS * ATT_W,
                       HEAD_DIM ** -0.5 * 1.4426950408889634, 1.0).astype(F32)
    w_in_b = (w_in * qscale).astype(BF16).reshape(depth, d, -1, CHUNK).swapaxes(1, 2)
    wpa_b, wpb_b, wo_b = (w.astype(BF16) for w in (w_pa, w_pb, w_o))
    wrg_bd = (0.5 * _block_diag(w_rg)).astype(BF16)
    wig_bd = (0.5 * _block_diag(w_ig)).astype(BF16)
    lam = lru_lambda.reshape(depth, -1, 128)
    xf = x.reshape(n, d)
    for l in range(depth):
        *a_g, rest, hb = _inproj_lru_call(
            xf, mod, row3(g_pre), w_in_b, conv_w, row3(conv_b), wrg_bd, 0.5 * row3(b_rg),
            wig_bd, 0.5 * row3(b_ig), lam, l, seq // tm_in, tm_in)
        os_, sts_ = [], []
        for g, (_, r) in enumerate(GROUPS):
            o, st = _attn_call(a_g[g], g, r, batch, seq, *ATTN_TILES[g])
            os_.append(o)
            sts_.append(st)
        xf = _post_call(xf, mod, os_, sts_, rest, hb, wpa_b, wpb_b, wo_b, row3(g_post),
                        l, seq // tm_post, tm_post)
    return xf.reshape(batch, seq, d)
```
